```python
import math
import jax, jax.numpy as jnp
from jax import lax
import numpy as np

D_MODEL = 2048
BATCH = 4
SEQ = 4096
DEPTH = 1

MEM_LEN = 256
EPS = 1e-6
HEAD_DIM_A = 128
WIDTH_A = (3 * D_MODEL) // 4
N_Q_HEADS_A = WIDTH_A // HEAD_DIM_A
N_KV_HEADS_A = N_Q_HEADS_A // 3
KV_WIDTH_A = N_KV_HEADS_A * HEAD_DIM_A
WINDOW = 128
BLOCK = 128
WIDTH_B = (3 * D_MODEL) // 4
CHUNK = 128
GROUP_DIM_B = 128
N_GROUPS_B = WIDTH_B // GROUP_DIM_B
N_HEADS_C = 4
WIDTH_C = D_MODEL // 2
HEAD_DIM_C = WIDTH_C // N_HEADS_C
N_BRANCH = 3
IN_SPLITS = [WIDTH_A, KV_WIDTH_A, KV_WIDTH_A, WIDTH_A,
             WIDTH_B, WIDTH_B, WIDTH_B,
             WIDTH_C, WIDTH_C,
             N_BRANCH * D_MODEL]
N_IN = sum(IN_SPLITS)
NEG_INF = -1e30

kernel_name = "hybrid_gated_window_gmlp_memxattn"


def rmsnorm(x, gain):
    xf = x.astype(jnp.float32)
    xf = xf * lax.rsqrt(jnp.mean(xf * xf, axis=-1, keepdims=True) + EPS)
    return (xf * gain.astype(jnp.float32)).astype(x.dtype)


def layernorm(x, gain, bias):
    xf = x.astype(jnp.float32)
    mu = jnp.mean(xf, axis=-1, keepdims=True)
    xc = xf - mu
    var = jnp.mean(xc * xc, axis=-1, keepdims=True)
    y = xc * lax.rsqrt(var + EPS) * gain.astype(jnp.float32) + bias.astype(jnp.float32)
    return y.astype(x.dtype)


def alibi_slopes(n):
    def pow2_slopes(m):
        start = 2.0 ** (-8.0 / m)
        return [start ** (i + 1) for i in range(m)]
    if math.log2(n).is_integer():
        s = pow2_slopes(n)
    else:
        c = 2 ** int(math.floor(math.log2(n)))
        s = pow2_slopes(c) + pow2_slopes(2 * c)[0::2][: n - c]
    return np.asarray(s, dtype=np.float32)


def windowed_gqa_sink(q, k, v, sink):
    B, S, Hq, Dh = q.shape
    Hkv = k.shape[2]
    rep = Hq // Hkv
    nb = S // BLOCK
    qb = q.reshape(B, nb, BLOCK, Hkv, rep, Dh)
    pad = ((0, 0), (BLOCK, BLOCK), (0, 0), (0, 0))
    kp = jnp.pad(k, pad).reshape(B, nb + 2, BLOCK, Hkv, Dh)
    vp = jnp.pad(v, pad).reshape(B, nb + 2, BLOCK, Hkv, Dh)
    kb = jnp.concatenate([kp[:, :-2], kp[:, 1:-1], kp[:, 2:]], axis=2)
    vb = jnp.concatenate([vp[:, :-2], vp[:, 1:-1], vp[:, 2:]], axis=2)
    scale = 1.0 / math.sqrt(Dh)
    scores = jnp.einsum('bnqhrd,bnkhd->bnhrqk', qb, kb).astype(jnp.float32) * scale
    t = jnp.arange(BLOCK)[:, None]
    j = jnp.arange(3 * BLOCK)[None, :]
    dist = jnp.abs(t - (j - BLOCK))
    key_pos = (jnp.arange(nb)[:, None] - 1) * BLOCK + jnp.arange(3 * BLOCK)[None, :]
    valid = (dist <= WINDOW)[None] & ((key_pos >= 0) & (key_pos < S))[:, None, :]
    slopes = jnp.asarray(alibi_slopes(Hq)).reshape(Hkv, rep)
    alibi = -slopes[:, :, None, None] * dist.astype(jnp.float32)[None, None]
    scores = scores + alibi[None, None]
    scores = jnp.where(valid[None, :, None, None], scores, NEG_INF)
    sink_col = jnp.broadcast_to(sink.astype(jnp.float32).reshape(1, 1, Hkv, rep, 1, 1),
                                scores.shape[:-1] + (1,))
    probs = jax.nn.softmax(jnp.concatenate([scores, sink_col], axis=-1), axis=-1)[..., :-1]
    out = jnp.einsum('bnhrqk,bnkhd->bnqhrd', probs.astype(v.dtype), vb)
    return out.reshape(B, S, Hq * Dh)


def chunked_spatial_gating(u, v, ln_gain, ln_bias, w_s, b_s):
    B, S, _ = u.shape
    u = jax.nn.gelu(u, approximate=False)
    v = layernorm(jax.nn.gelu(v, approximate=False), ln_gain, ln_bias)
    vc = v.reshape(B, S // CHUNK, CHUNK, N_GROUPS_B, GROUP_DIM_B)
    s = jnp.einsum('gts,bcsgd->bctgd', w_s, vc) + b_s.T[None, None, :, :, None]
    return u * s.reshape(B, S, WIDTH_B)


def memory_cross_attention(q, mem_h, w_kv_mem):
    B, S, _ = q.shape
    M = mem_h.shape[1]
    kv = mem_h @ w_kv_mem
    k, v = jnp.split(kv, 2, axis=-1)
    qh = q.reshape(B, S, N_HEADS_C, HEAD_DIM_C)
    kh = k.reshape(B, M, N_HEADS_C, HEAD_DIM_C)
    vh = v.reshape(B, M, N_HEADS_C, HEAD_DIM_C)
    scores = jnp.einsum('bshd,bmhd->bhsm', qh, kh).astype(jnp.float32) / math.sqrt(HEAD_DIM_C)
    probs = jax.nn.softmax(scores, axis=-1).astype(vh.dtype)
    out = jnp.einsum('bhsm,bmhd->bshd', probs, vh)
    return out.reshape(B, S, WIDTH_C)


def setup_inputs(seed: int = 0) -> dict:
    key = jax.random.key(seed)
    ks = jax.random.split(key, 20)
    f = jnp.float32
    nrm = lambda k, shape, scale: jax.random.normal(k, shape, f) * scale
    return {
        "x": nrm(ks[0], (BATCH, SEQ, D_MODEL), 1.0),
        "mem": nrm(ks[1], (BATCH, MEM_LEN, D_MODEL), 1.0),
        "norm_gain": 1.0 + nrm(ks[2], (DEPTH, D_MODEL), 0.02),
        "mem_norm_gain": 1.0 + nrm(ks[3], (DEPTH, D_MODEL), 0.02),
        "w_in": nrm(ks[4], (DEPTH, D_MODEL, N_IN), D_MODEL ** -0.5),
        "sink": nrm(ks[5], (DEPTH, N_Q_HEADS_A), 0.5),
        "ln_v_gain": 1.0 + nrm(ks[6], (DEPTH, WIDTH_B), 0.02),
        "ln_v_bias": nrm(ks[7], (DEPTH, WIDTH_B), 0.01),
        "w_spatial": nrm(ks[8], (DEPTH, N_GROUPS_B, CHUNK, CHUNK), CHUNK ** -0.5),
        "b_spatial": 1.0 + nrm(ks[9], (DEPTH, N_GROUPS_B, CHUNK), 0.01),
        "w_kv_mem": nrm(ks[10], (DEPTH, D_MODEL, 2 * WIDTH_C), D_MODEL ** -0.5),
        "w_br_a": nrm(ks[11], (DEPTH, WIDTH_A, D_MODEL), WIDTH_A ** -0.5),
        "w_br_b": nrm(ks[12], (DEPTH, WIDTH_B, D_MODEL), WIDTH_B ** -0.5),
        "w_br_c": nrm(ks[13], (DEPTH, WIDTH_C, D_MODEL), WIDTH_C ** -0.5),
        "w_out": nrm(ks[14], (DEPTH, D_MODEL, D_MODEL), D_MODEL ** -0.5),
        "final_gain": 1.0 + nrm(ks[15], (D_MODEL,), 0.02),
    }


def reference(x, mem, norm_gain, mem_norm_gain, w_in, sink, ln_v_gain, ln_v_bias,
              w_spatial, b_spatial, w_kv_mem, w_br_a, w_br_b, w_br_c, w_out, final_gain):
    B, S, D = x.shape
    split_idx = [int(c) for c in np.cumsum(IN_SPLITS)[:-1]]
    for l in range(DEPTH):
        h = rmsnorm(x, norm_gain[l])
        proj = h @ w_in[l]
        (q_a, k_a, v_a, z_a, u_b, v_b, z_b, q_c, z_c, gates) = jnp.split(proj, split_idx, axis=-1)
        o_a = windowed_gqa_sink(q_a.reshape(B, S, N_Q_HEADS_A, HEAD_DIM_A),
                                k_a.reshape(B, S, N_KV_HEADS_A, HEAD_DIM_A),
                                v_a.reshape(B, S, N_KV_HEADS_A, HEAD_DIM_A),
                                sink[l])
        o_a = o_a * jax.nn.silu(z_a)
        o_b = chunked_spatial_gating(u_b, v_b, ln_v_gain[l], ln_v_bias[l], w_spatial[l], b_spatial[l])
        o_b = o_b * jax.nn.silu(z_b)
        mem_h = rmsnorm(mem, mem_norm_gain[l])
        o_c = memory_cross_attention(q_c, mem_h, w_kv_mem[l]) * jax.nn.silu(z_c)
        g_a, g_b, g_c = jnp.split(jax.nn.sigmoid(gates), N_BRANCH, axis=-1)
        merged = (o_a @ w_br_a[l]) * g_a + (o_b @ w_br_b[l]) * g_b + (o_c @ w_br_c[l]) * g_c
        x = x + merged @ w_out[l]
    return rmsnorm(x, final_gain)
```

```python
import functools
import math

import jax
import jax.numpy as jnp
import numpy as np
from jax import lax
from jax.experimental import pallas as pl
from jax.experimental.pallas import tpu as pltpu

D_MODEL = 2048
MEM_LEN = 256
EPS = 1e-6
HEAD_DIM_A = 128
WIDTH_A = 1536
N_Q_HEADS_A = 12
N_KV_HEADS_A = 4
REP_A = N_Q_HEADS_A // N_KV_HEADS_A
KV_WIDTH_A = 512
BLOCK = 128
WIDTH_B = 1536
CHUNK = 128
N_GROUPS_B = 12
N_HEADS_C = 4
WIDTH_C = 1024
HEAD_DIM_C = 256
NEG_INF = -1e30

COL_BLOCK = 512
CB_Q_A, CB_K_A, CB_V_A, CB_Z_A = 0, 3, 4, 5
CB_U_B, CB_V_B, CB_Z_B = 8, 11, 14
CB_Q_C, CB_Z_C = 17, 19
CB_GATES = 21
N_CB_MAIN = 21
N_MAIN = N_CB_MAIN * COL_BLOCK
N_GATES = 3 * D_MODEL

SQRT_HALF = math.sqrt(0.5)
VMEM_LIMIT = 56 * 1024 * 1024


def _alibi_slopes(n):
    def pow2_slopes(m):
        start = 2.0 ** (-8.0 / m)
        return [start ** (i + 1) for i in range(m)]
    if math.log2(n).is_integer():
        s = pow2_slopes(n)
    else:
        c = 2 ** int(math.floor(math.log2(n)))
        s = pow2_slopes(c) + pow2_slopes(2 * c)[0::2][: n - c]
    return [float(v) for v in np.asarray(s, dtype=np.float32)]


ALIBI_SLOPES = _alibi_slopes(N_Q_HEADS_A)


def _sigmoid(x):
    return 0.5 * (1.0 + jnp.tanh(0.5 * x))


def _silu(x):
    return x * _sigmoid(x)


def _gelu(x):
    return 0.5 * x * (1.0 + lax.erf(x * SQRT_HALF))


def _dot(a, b):
    return jnp.dot(a, b, preferred_element_type=jnp.float32)


def _dot_nt(a, b):
    return lax.dot_general(a, b, (((1,), (1,)), ((), ())), preferred_element_type=jnp.float32)


def _rmsnorm_kernel(x_ref, g_ref, o_ref):
    x = x_ref[...]
    ms = jnp.mean(x * x, axis=-1, keepdims=True)
    o_ref[...] = ((x * lax.rsqrt(ms + EPS)) * g_ref[...]).astype(o_ref.dtype)


def _rmsnorm_bf16(x2d, gain, tm):
    t, d = x2d.shape
    return pl.pallas_call(
        _rmsnorm_kernel,
        grid=(t // tm,),
        in_specs=[pl.BlockSpec((tm, d), lambda i: (i, 0)),
                  pl.BlockSpec((1, d), lambda i: (0, 0))],
        out_specs=pl.BlockSpec((tm, d), lambda i: (i, 0)),
        out_shape=jax.ShapeDtypeStruct((t, d), jnp.bfloat16),
        compiler_params=pltpu.CompilerParams(dimension_semantics=("arbitrary",),
                                             vmem_limit_bytes=VMEM_LIMIT),
        name="rmsnorm_bf16",
    )(x2d, gain.reshape(1, d))


def _inproj_main_kernel(h_ref, w_ref, o_ref):
    j = pl.program_id(1)
    acc = _dot(h_ref[...], w_ref[...])

    is_silu = ((j >= CB_Z_A) & (j < CB_U_B)) | ((j >= CB_Z_B) & (j < CB_Q_C)) | (j >= CB_Z_C)
    is_gelu = (j >= CB_U_B) & (j < CB_Z_B)
    is_plain = jnp.logical_not(is_silu | is_gelu)

    @pl.when(is_plain)
    def _():
        o_ref[...] = acc.astype(o_ref.dtype)

    @pl.when(is_silu)
    def _():
        o_ref[...] = _silu(acc).astype(o_ref.dtype)

    @pl.when(is_gelu)
    def _():
        o_ref[...] = _gelu(acc).astype(o_ref.dtype)


def _inproj_gates_kernel(h_ref, w_ref, o_ref):
    o_ref[...] = _sigmoid(_dot(h_ref[...], w_ref[...])).astype(o_ref.dtype)


def _inproj(h, w_in_bf16, tm, tn):
    t, d = h.shape
    params = pltpu.CompilerParams(dimension_semantics=("arbitrary", "arbitrary"),
                                  vmem_limit_bytes=VMEM_LIMIT)
    main = pl.pallas_call(
        _inproj_main_kernel,
        grid=(t // tm, N_MAIN // COL_BLOCK),
        in_specs=[pl.BlockSpec((tm, d), lambda i, j: (i, 0)),
                  pl.BlockSpec((d, COL_BLOCK), lambda i, j: (0, j))],
        out_specs=pl.BlockSpec((tm, COL_BLOCK), lambda i, j: (i, j)),
        out_shape=jax.ShapeDtypeStruct((t, N_MAIN), jnp.bfloat16),
        compiler_params=params,
        name="inproj_main",
    )(h, w_in_bf16)
    assert N_MAIN % tn == 0 and N_GATES % tn == 0
    gate_cb0 = N_MAIN // tn
    gates = pl.pallas_call(
        _inproj_gates_kernel,
        grid=(t // tm, N_GATES // tn),
        in_specs=[pl.BlockSpec((tm, d), lambda i, j: (i, 0)),
                  pl.BlockSpec((d, tn), lambda i, j: (0, gate_cb0 + j))],
        out_specs=pl.BlockSpec((tm, tn), lambda i, j: (i, j)),
        out_shape=jax.ShapeDtypeStruct((t, N_GATES), jnp.bfloat16),
        compiler_params=params,
        name="inproj_gates",
    )(h, w_in_bf16)
    return main, gates


def _attn_kernel(sink_ref, q_ref, kp_ref, kc_ref, kn_ref, vp_ref, vc_ref, vn_ref,
                 z0_ref, z1_ref, z2_ref, o_ref):
    n = pl.program_id(1)
    nb = pl.num_programs(1)
    scale = 1.0 / math.sqrt(HEAD_DIM_A)

    row = lax.broadcasted_iota(jnp.int32, (BLOCK, BLOCK), 0)
    col = lax.broadcasted_iota(jnp.int32, (BLOCK, BLOCK), 1)
    dist_c = jnp.abs(row - col).astype(jnp.float32)
    dist_p = (row + BLOCK - col).astype(jnp.float32)
    dist_n = (BLOCK + col - row).astype(jnp.float32)
    ok_p = (col >= row) & (n > 0)
    ok_n = (col <= row) & (n < nb - 1)

    z_refs = (z0_ref, z1_ref, z2_ref)
    for hk in range(N_KV_HEADS_A):
        ksl = slice(hk * HEAD_DIM_A, (hk + 1) * HEAD_DIM_A)
        kp, kc, kn = kp_ref[:, ksl], kc_ref[:, ksl], kn_ref[:, ksl]
        vp, vc, vn = vp_ref[:, ksl], vc_ref[:, ksl], vn_ref[:, ksl]
        for r in range(REP_A):
            h = hk * REP_A + r
            slope = ALIBI_SLOPES[h]
            q = q_ref[:, h * HEAD_DIM_A:(h + 1) * HEAD_DIM_A]
            s_c = _dot_nt(q, kc) * scale - slope * dist_c
            s_p = jnp.where(ok_p, _dot_nt(q, kp) * scale - slope * dist_p, NEG_INF)
            s_n = jnp.where(ok_n, _dot_nt(q, kn) * scale - slope * dist_n, NEG_INF)
            sink = sink_ref[h]
            m = jnp.maximum(jnp.maximum(jnp.max(s_c, axis=-1, keepdims=True),
                                        jnp.max(s_p, axis=-1, keepdims=True)),
                            jnp.maximum(jnp.max(s_n, axis=-1, keepdims=True), sink))
            p_c = jnp.exp(s_c - m)
            p_p = jnp.exp(s_p - m)
            p_n = jnp.exp(s_n - m)
            denom = (jnp.sum(p_c, axis=-1, keepdims=True) + jnp.sum(p_p, axis=-1, keepdims=True)
                     + jnp.sum(p_n, axis=-1, keepdims=True) + jnp.exp(sink - m))
            o = (_dot(p_c.astype(jnp.bfloat16), vc) + _dot(p_p.astype(jnp.bfloat16), vp)
                 + _dot(p_n.astype(jnp.bfloat16), vn))
            o = o / denom
            zc = (h * HEAD_DIM_A) % COL_BLOCK
            z = z_refs[(h * HEAD_DIM_A) // COL_BLOCK][:, zc:zc + HEAD_DIM_A].astype(jnp.float32)
            o_ref[:, h * HEAD_DIM_A:(h + 1) * HEAD_DIM_A] = (o * z).astype(o_ref.dtype)


def _branch_a(proj3, sink):
    b, s, _ = proj3.shape
    nb = s // BLOCK

    def blk(width, cb, shift):
        def imap(bi, n, sink_ref):
            return (bi, jnp.clip(n + shift, 0, nb - 1), cb)
        return pl.BlockSpec((None, BLOCK, width), imap)

    in_specs = [
        blk(WIDTH_A, 0, 0),
        blk(KV_WIDTH_A, CB_K_A, -1), blk(KV_WIDTH_A, CB_K_A, 0), blk(KV_WIDTH_A, CB_K_A, 1),
        blk(KV_WIDTH_A, CB_V_A, -1), blk(KV_WIDTH_A, CB_V_A, 0), blk(KV_WIDTH_A, CB_V_A, 1),
        blk(COL_BLOCK, CB_Z_A, 0), blk(COL_BLOCK, CB_Z_A + 1, 0), blk(COL_BLOCK, CB_Z_A + 2, 0),
    ]
    return pl.pallas_call(
        _attn_kernel,
        grid_spec=pltpu.PrefetchScalarGridSpec(
            num_scalar_prefetch=1,
            grid=(b, nb),
            in_specs=in_specs,
            out_specs=pl.BlockSpec((None, BLOCK, WIDTH_A), lambda bi, n, sink_ref: (bi, n, 0)),
        ),
        out_shape=jax.ShapeDtypeStruct((b, s, WIDTH_A), jnp.bfloat16),
        compiler_params=pltpu.CompilerParams(dimension_semantics=("arbitrary", "arbitrary"),
                                             vmem_limit_bytes=VMEM_LIMIT),
        name="branch_a_attn",
    )(sink, *([proj3] * 10))


def _branch_b_kernel(u0, u1, u2, v0, v1, v2, z0, z1, z2, lng_ref, lnb_ref, ws_ref, bs_ref, o_ref):
    tm = o_ref.shape[0]
    nc = tm // CHUNK
    gv = jnp.concatenate([v0[...], v1[...], v2[...]], axis=1).astype(jnp.float32)
    mu = jnp.mean(gv, axis=-1, keepdims=True)
    xc = gv - mu
    var = jnp.mean(xc * xc, axis=-1, keepdims=True)
    vn = (xc * lax.rsqrt(var + EPS) * lng_ref[...] + lnb_ref[...]).astype(jnp.bfloat16)
    u_refs = (u0, u1, u2)
    z_refs = (z0, z1, z2)
    for g in range(N_GROUPS_B):
        gsl = slice(g * CHUNK, (g + 1) * CHUNK)
        rhs = jnp.concatenate([vn[c * CHUNK:(c + 1) * CHUNK, gsl] for c in range(nc)], axis=1)
        sg = _dot(ws_ref[g], rhs)
        bias = bs_ref[g]
        cb, off = divmod(g * CHUNK, COL_BLOCK)
        for c in range(nc):
            rows = slice(c * CHUNK, (c + 1) * CHUNK)
            s_c = sg[:, c * CHUNK:(c + 1) * CHUNK] + bias
            u = u_refs[cb][rows, off:off + CHUNK].astype(jnp.float32)
            z = z_refs[cb][rows, off:off + CHUNK].astype(jnp.float32)
            o_ref[rows, gsl] = (u * s_c * z).astype(o_ref.dtype)


def _branch_b(proj, ln_gain, ln_bias, w_spatial_bf16, b_spatial_bcast, tm):
    t = proj.shape[0]

    def blk(cb):
        return pl.BlockSpec((tm, COL_BLOCK), lambda i: (i, cb))

    in_specs = ([blk(CB_U_B + k) for k in range(3)] + [blk(CB_V_B + k) for k in range(3)]
                + [blk(CB_Z_B + k) for k in range(3)]
                + [pl.BlockSpec((1, WIDTH_B), lambda i: (0, 0)),
                   pl.BlockSpec((1, WIDTH_B), lambda i: (0, 0)),
                   pl.BlockSpec((N_GROUPS_B, CHUNK, CHUNK), lambda i: (0, 0, 0)),
                   pl.BlockSpec((N_GROUPS_B, CHUNK, CHUNK), lambda i: (0, 0, 0))])
    return pl.pallas_call(
        _branch_b_kernel,
        grid=(t // tm,),
        in_specs=in_specs,
        out_specs=pl.BlockSpec((tm, WIDTH_B), lambda i: (i, 0)),
        out_shape=jax.ShapeDtypeStruct((t, WIDTH_B), jnp.bfloat16),
        compiler_params=pltpu.CompilerParams(dimension_semantics=("arbitrary",),
                                             vmem_limit_bytes=VMEM_LIMIT),
        name="branch_b_gmlp",
    )(*([proj] * 9), ln_gain.reshape(1, WIDTH_B), ln_bias.reshape(1, WIDTH_B),
      w_spatial_bf16, b_spatial_bcast)


def _matmul_kernel(a_ref, b_ref, o_ref):
    o_ref[...] = _dot(a_ref[...], b_ref[...]).astype(o_ref.dtype)


def _mem_kv(mem_h, w_kv_bf16, tn):
    m, d = mem_h.shape
    n = w_kv_bf16.shape[1]
    return pl.pallas_call(
        _matmul_kernel,
        grid=(n // tn,),
        in_specs=[pl.BlockSpec((m, d), lambda j: (0, 0)),
                  pl.BlockSpec((d, tn), lambda j: (0, j))],
        out_specs=pl.BlockSpec((m, tn), lambda j: (0, j)),
        out_shape=jax.ShapeDtypeStruct((m, n), jnp.bfloat16),
        compiler_params=pltpu.CompilerParams(dimension_semantics=("arbitrary",),
                                             vmem_limit_bytes=VMEM_LIMIT),
        name="mem_kv",
    )(mem_h, w_kv_bf16)


def _branch_c_kernel(q0, q1, z0, z1, kv_ref, o_ref):
    scale = 1.0 / math.sqrt(HEAD_DIM_C)
    q_refs = (q0, q1)
    z_refs = (z0, z1)
    for h in range(N_HEADS_C):
        cb, off = divmod(h * HEAD_DIM_C, COL_BLOCK)
        q = q_refs[cb][:, off:off + HEAD_DIM_C]
        k = kv_ref[:, h * HEAD_DIM_C:(h + 1) * HEAD_DIM_C]
        v = kv_ref[:, WIDTH_C + h * HEAD_DIM_C:WIDTH_C + (h + 1) * HEAD_DIM_C]
        s = _dot_nt(q, k) * scale
        m = jnp.max(s, axis=-1, keepdims=True)
        p = jnp.exp(s - m)
        denom = jnp.sum(p, axis=-1, keepdims=True)
        o = _dot(p.astype(jnp.bfloat16), v) / denom
        z = z_refs[cb][:, off:off + HEAD_DIM_C].astype(jnp.float32)
        o_ref[:, h * HEAD_DIM_C:(h + 1) * HEAD_DIM_C] = (o * z).astype(o_ref.dtype)


def _branch_c(proj3, kv3, tm):
    b, s, _ = proj3.shape

    def blk(cb):
        return pl.BlockSpec((None, tm, COL_BLOCK), lambda bi, i: (bi, i, cb))

    in_specs = [blk(CB_Q_C), blk(CB_Q_C + 1), blk(CB_Z_C), blk(CB_Z_C + 1),
                pl.BlockSpec((None, MEM_LEN, 2 * WIDTH_C), lambda bi, i: (bi, 0, 0))]
    return pl.pallas_call(
        _branch_c_kernel,
        grid=(b, s // tm),
        in_specs=in_specs,
        out_specs=pl.BlockSpec((None, tm, WIDTH_C), lambda bi, i: (bi, i, 0)),
        out_shape=jax.ShapeDtypeStruct((b, s, WIDTH_C), jnp.bfloat16),
        compiler_params=pltpu.CompilerParams(dimension_semantics=("arbitrary", "arbitrary"),
                                             vmem_limit_bytes=VMEM_LIMIT),
        name="branch_c_xattn",
    )(proj3, proj3, proj3, proj3, kv3)


def _merge_kernel(x_ref, ga_ref, gb_ref, gc_ref, gates_ref, wa_ref, wb_ref, wc_ref, wo_ref,
                  fg_ref, o_ref, *, final_norm):
    d = D_MODEL
    merged = _dot(ga_ref[...], wa_ref[...]) * gates_ref[:, 0:d].astype(jnp.float32)
    merged = merged + _dot(gb_ref[...], wb_ref[...]) * gates_ref[:, d:2 * d].astype(jnp.float32)
    merged = merged + _dot(gc_ref[...], wc_ref[...]) * gates_ref[:, 2 * d:3 * d].astype(jnp.float32)
    y = x_ref[...] + _dot(merged.astype(jnp.bfloat16), wo_ref[...])
    if final_norm:
        ms = jnp.mean(y * y, axis=-1, keepdims=True)
        y = (y * lax.rsqrt(ms + EPS)) * fg_ref[...]
    o_ref[...] = y


def _merge(x2d, ga, gb, gc, gates, wa, wb, wc, wo, final_gain, tm, final_norm):
    t, d = x2d.shape

    def rows(width):
        return pl.BlockSpec((tm, width), lambda i: (i, 0))

    def resident(shape):
        return pl.BlockSpec(shape, lambda i: (0, 0), pipeline_mode=pl.Buffered(1))

    return pl.pallas_call(
        functools.partial(_merge_kernel, final_norm=final_norm),
        grid=(t // tm,),
        in_specs=[rows(d), rows(WIDTH_A), rows(WIDTH_B), rows(WIDTH_C), rows(N_GATES),
                  resident((WIDTH_A, d)), resident((WIDTH_B, d)), resident((WIDTH_C, d)),
                  resident((d, d)), resident((1, d))],
        out_specs=rows(d),
        out_shape=jax.ShapeDtypeStruct((t, d), jnp.float32),
        compiler_params=pltpu.CompilerParams(dimension_semantics=("arbitrary",),
                                             vmem_limit_bytes=VMEM_LIMIT),
        name="merge_out",
    )(x2d, ga, gb, gc, gates, wa, wb, wc, wo, final_gain.reshape(1, d))


def kernel(x, mem, norm_gain, mem_norm_gain, w_in, sink, ln_v_gain, ln_v_bias, w_spatial, b_spatial,
           w_kv_mem, w_br_a, w_br_b, w_br_c, w_out, final_gain):
    b, s, d = x.shape
    depth = w_in.shape[0]
    bf16 = jnp.bfloat16
    x2d = x.reshape(b * s, d)
    mem2d = mem.reshape(b * MEM_LEN, d)
    for l in range(depth):
        h = _rmsnorm_bf16(x2d, norm_gain[l], 512)
        proj, gates = _inproj(h, w_in[l].astype(bf16), 2048, 512)
        proj3 = proj.reshape(b, s, N_MAIN)

        ga = _branch_a(proj3, sink[l].astype(jnp.float32)).reshape(b * s, WIDTH_A)

        bs_bcast = jnp.broadcast_to(b_spatial[l].astype(jnp.float32)[:, :, None],
                                    (N_GROUPS_B, CHUNK, CHUNK))
        gb = _branch_b(proj, ln_v_gain[l], ln_v_bias[l], w_spatial[l].astype(bf16), bs_bcast, 512)

        mem_h = _rmsnorm_bf16(mem2d, mem_norm_gain[l], 256)
        kv = _mem_kv(mem_h, w_kv_mem[l].astype(bf16), 512)
        gc = _branch_c(proj3, kv.reshape(b, MEM_LEN, 2 * WIDTH_C), 512).reshape(b * s, WIDTH_C)

        x2d = _merge(x2d, ga, gb, gc, gates, w_br_a[l].astype(bf16), w_br_b[l].astype(bf16),
                     w_br_c[l].astype(bf16), w_out[l].astype(bf16), final_gain, 256,
                     final_norm=(l == depth - 1))
    return x2d.reshape(b, s, d)
```

```python
import functools
import math

import jax
import jax.numpy as jnp
import numpy as np
from jax import lax
from jax.experimental import pallas as pl
from jax.experimental.pallas import tpu as pltpu

D_MODEL = 2048
MEM_LEN = 256
EPS = 1e-6
HEAD_DIM_A = 128
WIDTH_A = 1536
N_Q_HEADS_A = 12
N_KV_HEADS_A = 4
REP_A = N_Q_HEADS_A // N_KV_HEADS_A
KV_WIDTH_A = 512
BLOCK = 128
WIDTH_B = 1536
CHUNK = 128
N_GROUPS_B = 12
N_HEADS_C = 4
WIDTH_C = 1024
HEAD_DIM_C = 256
NEG_INF = -1e30

COL_BLOCK = 512
CB_Q_A, CB_K_A, CB_V_A, CB_Z_A = 0, 3, 4, 5
CB_U_B, CB_V_B, CB_Z_B = 8, 11, 14
CB_Q_C, CB_Z_C = 17, 19
CB_GATES = 21
N_CB_MAIN = 21
N_MAIN = N_CB_MAIN * COL_BLOCK
N_GATES = 3 * D_MODEL

SQRT_HALF = math.sqrt(0.5)
LOG2_E = math.log2(math.e)
VMEM_LIMIT = 56 * 1024 * 1024


def _alibi_slopes(n):
    def pow2_slopes(m):
        start = 2.0 ** (-8.0 / m)
        return [start ** (i + 1) for i in range(m)]
    if math.log2(n).is_integer():
        s = pow2_slopes(n)
    else:
        c = 2 ** int(math.floor(math.log2(n)))
        s = pow2_slopes(c) + pow2_slopes(2 * c)[0::2][: n - c]
    return [float(v) for v in np.asarray(s, dtype=np.float32)]


ALIBI_SLOPES = _alibi_slopes(N_Q_HEADS_A)


def _sigmoid(x):
    return 0.5 * (1.0 + jnp.tanh(0.5 * x))


def _silu(x):
    return x * _sigmoid(x)


def _gelu(x):
    return 0.5 * x * (1.0 + lax.erf(x * SQRT_HALF))


def _dot(a, b):
    return jnp.dot(a, b, preferred_element_type=jnp.float32)


def _dot_nt(a, b):
    return lax.dot_general(a, b, (((1,), (1,)), ((), ())), preferred_element_type=jnp.float32)


def _rmsnorm_kernel(x_ref, g_ref, o_ref):
    x = x_ref[...]
    ms = jnp.mean(x * x, axis=-1, keepdims=True)
    o_ref[...] = ((x * lax.rsqrt(ms + EPS)) * g_ref[...]).astype(o_ref.dtype)


def _rmsnorm_bf16(x2d, gain, tm):
    t, d = x2d.shape
    return pl.pallas_call(
        _rmsnorm_kernel,
        grid=(t // tm,),
        in_specs=[pl.BlockSpec((tm, d), lambda i: (i, 0)),
                  pl.BlockSpec((1, d), lambda i: (0, 0))],
        out_specs=pl.BlockSpec((tm, d), lambda i: (i, 0)),
        out_shape=jax.ShapeDtypeStruct((t, d), jnp.bfloat16),
        compiler_params=pltpu.CompilerParams(dimension_semantics=("arbitrary",),
                                             vmem_limit_bytes=VMEM_LIMIT),
        name="rmsnorm_bf16",
    )(x2d, gain.reshape(1, d))


def _inproj_main_kernel(h_ref, w_ref, o_ref):
    j = pl.program_id(1)
    acc = _dot(h_ref[...], w_ref[...])

    is_silu = ((j >= CB_Z_A) & (j < CB_U_B)) | ((j >= CB_Z_B) & (j < CB_Q_C)) | (j >= CB_Z_C)
    is_gelu = (j >= CB_U_B) & (j < CB_Z_B)
    is_plain = jnp.logical_not(is_silu | is_gelu)

    @pl.when(is_plain)
    def _():
        o_ref[...] = acc.astype(o_ref.dtype)

    @pl.when(is_silu)
    def _():
        o_ref[...] = _silu(acc).astype(o_ref.dtype)

    @pl.when(is_gelu)
    def _():
        o_ref[...] = _gelu(acc).astype(o_ref.dtype)


def _inproj_gates_kernel(h_ref, w_ref, o_ref):
    o_ref[...] = _sigmoid(_dot(h_ref[...], w_ref[...])).astype(o_ref.dtype)


def _inproj(h, w_in_bf16, tm, tn):
    t, d = h.shape
    params = pltpu.CompilerParams(dimension_semantics=("arbitrary", "arbitrary"),
                                  vmem_limit_bytes=VMEM_LIMIT)
    main = pl.pallas_call(
        _inproj_main_kernel,
        grid=(t // tm, N_MAIN // COL_BLOCK),
        in_specs=[pl.BlockSpec((tm, d), lambda i, j: (i, 0)),
                  pl.BlockSpec((d, COL_BLOCK), lambda i, j: (0, j))],
        out_specs=pl.BlockSpec((tm, COL_BLOCK), lambda i, j: (i, j)),
        out_shape=jax.ShapeDtypeStruct((t, N_MAIN), jnp.bfloat16),
        compiler_params=params,
        name="inproj_main",
    )(h, w_in_bf16)
    assert N_MAIN % tn == 0 and N_GATES % tn == 0
    gate_cb0 = N_MAIN // tn
    gates = pl.pallas_call(
        _inproj_gates_kernel,
        grid=(t // tm, N_GATES // tn),
        in_specs=[pl.BlockSpec((tm, d), lambda i, j: (i, 0)),
                  pl.BlockSpec((d, tn), lambda i, j: (0, gate_cb0 + j))],
        out_specs=pl.BlockSpec((tm, tn), lambda i, j: (i, j)),
        out_shape=jax.ShapeDtypeStruct((t, N_GATES), jnp.bfloat16),
        compiler_params=params,
        name="inproj_gates",
    )(h, w_in_bf16)
    return main, gates


def _attn_kernel(sink_ref, q_ref, kp_ref, kc_ref, kn_ref, vp_ref, vc_ref, vn_ref,
                 z0_ref, z1_ref, z2_ref, o_ref, kbuf, vbuf, bias_ref):
    tq = q_ref.shape[0]
    nblk = tq // BLOCK
    bi = pl.program_id(0)
    n = pl.program_id(1)
    nt = pl.num_programs(1)
    c1 = LOG2_E / math.sqrt(HEAD_DIM_A)

    @pl.when((bi == 0) & (n == 0))
    def _():
        row = lax.broadcasted_iota(jnp.int32, (BLOCK, 3 * BLOCK), 0)
        col = lax.broadcasted_iota(jnp.int32, (BLOCK, 3 * BLOCK), 1)
        dist = jnp.abs(row - (col - BLOCK))
        valid = dist <= BLOCK
        distf = dist.astype(jnp.float32)
        for h in range(N_Q_HEADS_A):
            hk, r = divmod(h, REP_A)
            bias_ref[hk, r * BLOCK:(r + 1) * BLOCK, :] = jnp.where(
                valid, (-ALIBI_SLOPES[h] * LOG2_E) * distf, NEG_INF)
        for hk in range(N_KV_HEADS_A):
            vbuf[:, (2 * hk + 1) * HEAD_DIM_A:(2 * hk + 2) * HEAD_DIM_A] = jnp.ones(
                (tq + 2 * BLOCK, HEAD_DIM_A), vbuf.dtype)

    kbuf[0:BLOCK, :] = kp_ref[...]
    kbuf[BLOCK:BLOCK + tq, :] = kc_ref[...]
    kbuf[BLOCK + tq:, :] = kn_ref[...]
    for hk in range(N_KV_HEADS_A):
        src = slice(hk * HEAD_DIM_A, (hk + 1) * HEAD_DIM_A)
        dst = slice(2 * hk * HEAD_DIM_A, (2 * hk + 1) * HEAD_DIM_A)
        vbuf[0:BLOCK, dst] = vp_ref[:, src]
        vbuf[BLOCK:BLOCK + tq, dst] = vc_ref[:, src]
        vbuf[BLOCK + tq:, dst] = vn_ref[:, src]

    edge_p = jnp.where(n == 0, NEG_INF, 0.0)
    edge_n = jnp.where(n == nt - 1, NEG_INF, 0.0)

    z_refs = (z0_ref, z1_ref, z2_ref)
    units = [(blk, hk) for blk in range(nblk) for hk in range(N_KV_HEADS_A)]

    def scores(unit):
        blk, hk = unit
        rows = slice(blk * BLOCK, (blk + 1) * BLOCK)
        q = jnp.concatenate(
            [q_ref[rows, (hk * REP_A + r) * HEAD_DIM_A:(hk * REP_A + r + 1) * HEAD_DIM_A]
             for r in range(REP_A)], axis=0)
        k = kbuf[blk * BLOCK:(blk + 3) * BLOCK, hk * HEAD_DIM_A:(hk + 1) * HEAD_DIM_A]
        return _dot_nt(q, k)

    s_next = scores(units[0])
    for i, (blk, hk) in enumerate(units):
        s = s_next
        if i + 1 < len(units):
            s_next = scores(units[i + 1])
        rows = slice(blk * BLOCK, (blk + 1) * BLOCK)
        krows = slice(blk * BLOCK, (blk + 3) * BLOCK)
        t = s * c1 + bias_ref[hk]
        t_p, t_c, t_n = t[:, 0:BLOCK], t[:, BLOCK:2 * BLOCK], t[:, 2 * BLOCK:]
        if blk == 0:
            t_p = t_p + edge_p
        if blk == nblk - 1:
            t_n = t_n + edge_n
        sink2 = jnp.concatenate(
            [jnp.full((BLOCK, BLOCK), sink_ref[hk * REP_A + r] * LOG2_E, jnp.float32)
             for r in range(REP_A)], axis=0)
        m = jnp.max(jnp.maximum(jnp.maximum(t_p, t_c), t_n), axis=-1, keepdims=True)
        m = jnp.maximum(jnp.broadcast_to(m, (REP_A * BLOCK, BLOCK)), sink2)
        p = jnp.concatenate([jnp.exp2(t_p - m), jnp.exp2(t_c - m), jnp.exp2(t_n - m)],
                            axis=1).astype(jnp.bfloat16)
        ov = _dot(p, vbuf[krows, 2 * hk * HEAD_DIM_A:(2 * hk + 2) * HEAD_DIM_A])
        denom = ov[:, HEAD_DIM_A:] + jnp.exp2(sink2 - m)
        o = ov[:, :HEAD_DIM_A] * (1.0 / denom)
        for r in range(REP_A):
            h = hk * REP_A + r
            cb, off = divmod(h * HEAD_DIM_A, COL_BLOCK)
            z = z_refs[cb][rows, off:off + HEAD_DIM_A].astype(jnp.float32)
            o_ref[rows, h * HEAD_DIM_A:(h + 1) * HEAD_DIM_A] = (
                o[r * BLOCK:(r + 1) * BLOCK] * z).astype(o_ref.dtype)


def _branch_a(proj3, sink, tq):
    b, s, _ = proj3.shape
    nb = s // BLOCK
    per = tq // BLOCK

    def tile(width, cb):
        return pl.BlockSpec((None, tq, width), lambda bi, n, sink_ref: (bi, n, cb))

    def halo(cb, after):
        def imap(bi, n, sink_ref):
            blk = (n + 1) * per if after else n * per - 1
            return (bi, jnp.clip(blk, 0, nb - 1), cb)
        return pl.BlockSpec((None, BLOCK, KV_WIDTH_A), imap)

    in_specs = [
        tile(WIDTH_A, 0),
        halo(CB_K_A, False), tile(KV_WIDTH_A, CB_K_A), halo(CB_K_A, True),
        halo(CB_V_A, False), tile(KV_WIDTH_A, CB_V_A), halo(CB_V_A, True),
        tile(COL_BLOCK, CB_Z_A), tile(COL_BLOCK, CB_Z_A + 1), tile(COL_BLOCK, CB_Z_A + 2),
    ]
    return pl.pallas_call(
        _attn_kernel,
        grid_spec=pltpu.PrefetchScalarGridSpec(
            num_scalar_prefetch=1,
            grid=(b, s // tq),
            in_specs=in_specs,
            out_specs=pl.BlockSpec((None, tq, WIDTH_A), lambda bi, n, sink_ref: (bi, n, 0)),
            scratch_shapes=[pltpu.VMEM((tq + 2 * BLOCK, KV_WIDTH_A), jnp.bfloat16),
                            pltpu.VMEM((tq + 2 * BLOCK, 2 * KV_WIDTH_A), jnp.bfloat16),
                            pltpu.VMEM((N_KV_HEADS_A, REP_A * BLOCK, 3 * BLOCK), jnp.float32)],
        ),
        out_shape=jax.ShapeDtypeStruct((b, s, WIDTH_A), jnp.bfloat16),
        compiler_params=pltpu.CompilerParams(dimension_semantics=("arbitrary", "arbitrary"),
                                             vmem_limit_bytes=VMEM_LIMIT),
        name="branch_a_attn",
    )(sink, *([proj3] * 10))


def _branch_b_kernel(u0, u1, u2, v0, v1, v2, z0, z1, z2, lng_ref, lnb_ref, ws_ref, bs_ref, o_ref):
    tm = o_ref.shape[0]
    nc = tm // CHUNK
    gv = jnp.concatenate([v0[...], v1[...], v2[...]], axis=1).astype(jnp.float32)
    mu = jnp.mean(gv, axis=-1, keepdims=True)
    xc = gv - mu
    var = jnp.mean(xc * xc, axis=-1, keepdims=True)
    vn = (xc * lax.rsqrt(var + EPS) * lng_ref[...] + lnb_ref[...]).astype(jnp.bfloat16)
    u_refs = (u0, u1, u2)
    z_refs = (z0, z1, z2)
    for g in range(N_GROUPS_B):
        gsl = slice(g * CHUNK, (g + 1) * CHUNK)
        rhs = jnp.concatenate([vn[c * CHUNK:(c + 1) * CHUNK, gsl] for c in range(nc)], axis=1)
        sg = _dot(ws_ref[g], rhs)
        bias = bs_ref[g]
        cb, off = divmod(g * CHUNK, COL_BLOCK)
        for c in range(nc):
            rows = slice(c * CHUNK, (c + 1) * CHUNK)
            s_c = sg[:, c * CHUNK:(c + 1) * CHUNK] + bias
            u = u_refs[cb][rows, off:off + CHUNK].astype(jnp.float32)
            z = z_refs[cb][rows, off:off + CHUNK].astype(jnp.float32)
            o_ref[rows, gsl] = (u * s_c * z).astype(o_ref.dtype)


def _branch_b(proj, ln_gain, ln_bias, w_spatial_bf16, b_spatial_bcast, tm):
    t = proj.shape[0]

    def blk(cb):
        return pl.BlockSpec((tm, COL_BLOCK), lambda i: (i, cb))

    in_specs = ([blk(CB_U_B + k) for k in range(3)] + [blk(CB_V_B + k) for k in range(3)]
                + [blk(CB_Z_B + k) for k in range(3)]
                + [pl.BlockSpec((1, WIDTH_B), lambda i: (0, 0)),
                   pl.BlockSpec((1, WIDTH_B), lambda i: (0, 0)),
                   pl.BlockSpec((N_GROUPS_B, CHUNK, CHUNK), lambda i: (0, 0, 0)),
                   pl.BlockSpec((N_GROUPS_B, CHUNK, CHUNK), lambda i: (0, 0, 0))])
    return pl.pallas_call(
        _branch_b_kernel,
        grid=(t // tm,),
        in_specs=in_specs,
        out_specs=pl.BlockSpec((tm, WIDTH_B), lambda i: (i, 0)),
        out_shape=jax.ShapeDtypeStruct((t, WIDTH_B), jnp.bfloat16),
        compiler_params=pltpu.CompilerParams(dimension_semantics=("arbitrary",),
                                             vmem_limit_bytes=VMEM_LIMIT),
        name="branch_b_gmlp",
    )(*([proj] * 9), ln_gain.reshape(1, WIDTH_B), ln_bias.reshape(1, WIDTH_B),
      w_spatial_bf16, b_spatial_bcast)


def _matmul_kernel(a_ref, b_ref, o_ref):
    o_ref[...] = _dot(a_ref[...], b_ref[...]).astype(o_ref.dtype)


def _mem_kv(mem_h, w_kv_bf16, tn):
    m, d = mem_h.shape
    n = w_kv_bf16.shape[1]
    return pl.pallas_call(
        _matmul_kernel,
        grid=(n // tn,),
        in_specs=[pl.BlockSpec((m, d), lambda j: (0, 0)),
                  pl.BlockSpec((d, tn), lambda j: (0, j))],
        out_specs=pl.BlockSpec((m, tn), lambda j: (0, j)),
        out_shape=jax.ShapeDtypeStruct((m, n), jnp.bfloat16),
        compiler_params=pltpu.CompilerParams(dimension_semantics=("arbitrary",),
                                             vmem_limit_bytes=VMEM_LIMIT),
        name="mem_kv",
    )(mem_h, w_kv_bf16)


def _branch_c_kernel(q0, q1, z0, z1, kv_ref, o_ref):
    scale = 1.0 / math.sqrt(HEAD_DIM_C)
    q_refs = (q0, q1)
    z_refs = (z0, z1)
    for h in range(N_HEADS_C):
        cb, off = divmod(h * HEAD_DIM_C, COL_BLOCK)
        q = q_refs[cb][:, off:off + HEAD_DIM_C]
        k = kv_ref[:, h * HEAD_DIM_C:(h + 1) * HEAD_DIM_C]
        v = kv_ref[:, WIDTH_C + h * HEAD_DIM_C:WIDTH_C + (h + 1) * HEAD_DIM_C]
        s = _dot_nt(q, k) * scale
        m = jnp.max(s, axis=-1, keepdims=True)
        p = jnp.exp(s - m)
        denom = jnp.sum(p, axis=-1, keepdims=True)
        o = _dot(p.astype(jnp.bfloat16), v) / denom
        z = z_refs[cb][:, off:off + HEAD_DIM_C].astype(jnp.float32)
        o_ref[:, h * HEAD_DIM_C:(h + 1) * HEAD_DIM_C] = (o * z).astype(o_ref.dtype)


def _branch_c(proj3, kv3, tm):
    b, s, _ = proj3.shape

    def blk(cb):
        return pl.BlockSpec((None, tm, COL_BLOCK), lambda bi, i: (bi, i, cb))

    in_specs = [blk(CB_Q_C), blk(CB_Q_C + 1), blk(CB_Z_C), blk(CB_Z_C + 1),
                pl.BlockSpec((None, MEM_LEN, 2 * WIDTH_C), lambda bi, i: (bi, 0, 0))]
    return pl.pallas_call(
        _branch_c_kernel,
        grid=(b, s // tm),
        in_specs=in_specs,
        out_specs=pl.BlockSpec((None, tm, WIDTH_C), lambda bi, i: (bi, i, 0)),
        out_shape=jax.ShapeDtypeStruct((b, s, WIDTH_C), jnp.bfloat16),
        compiler_params=pltpu.CompilerParams(dimension_semantics=("arbitrary", "arbitrary"),
                                             vmem_limit_bytes=VMEM_LIMIT),
        name="branch_c_xattn",
    )(proj3, proj3, proj3, proj3, kv3)


def _merge_kernel(x_ref, ga_ref, gb_ref, gc_ref, gates_ref, wa_ref, wb_ref, wc_ref, wo_ref,
                  fg_ref, o_ref, *, final_norm):
    d = D_MODEL
    merged = _dot(ga_ref[...], wa_ref[...]) * gates_ref[:, 0:d].astype(jnp.float32)
    merged = merged + _dot(gb_ref[...], wb_ref[...]) * gates_ref[:, d:2 * d].astype(jnp.float32)
    merged = merged + _dot(gc_ref[...], wc_ref[...]) * gates_ref[:, 2 * d:3 * d].astype(jnp.float32)
    y = x_ref[...] + _dot(merged.astype(jnp.bfloat16), wo_ref[...])
    if final_norm:
        ms = jnp.mean(y * y, axis=-1, keepdims=True)
        y = (y * lax.rsqrt(ms + EPS)) * fg_ref[...]
    o_ref[...] = y


def _merge(x2d, ga, gb, gc, gates, wa, wb, wc, wo, final_gain, tm, final_norm):
    t, d = x2d.shape

    def rows(width):
        return pl.BlockSpec((tm, width), lambda i: (i, 0))

    def resident(shape):
        return pl.BlockSpec(shape, lambda i: (0, 0), pipeline_mode=pl.Buffered(1))

    return pl.pallas_call(
        functools.partial(_merge_kernel, final_norm=final_norm),
        grid=(t // tm,),
        in_specs=[rows(d), rows(WIDTH_A), rows(WIDTH_B), rows(WIDTH_C), rows(N_GATES),
                  resident((WIDTH_A, d)), resident((WIDTH_B, d)), resident((WIDTH_C, d)),
                  resident((d, d)), resident((1, d))],
        out_specs=rows(d),
        out_shape=jax.ShapeDtypeStruct((t, d), jnp.float32),
        compiler_params=pltpu.CompilerParams(dimension_semantics=("arbitrary",),
                                             vmem_limit_bytes=VMEM_LIMIT),
        name="merge_out",
    )(x2d, ga, gb, gc, gates, wa, wb, wc, wo, final_gain.reshape(1, d))


def kernel(x, mem, norm_gain, mem_norm_gain, w_in, sink, ln_v_gain, ln_v_bias, w_spatial, b_spatial,
           w_kv_mem, w_br_a, w_br_b, w_br_c, w_out, final_gain):
    b, s, d = x.shape
    depth = w_in.shape[0]
    bf16 = jnp.bfloat16
    x2d = x.reshape(b * s, d)
    mem2d = mem.reshape(b * MEM_LEN, d)
    for l in range(depth):
        h = _rmsnorm_bf16(x2d, norm_gain[l], 512)
        proj, gates = _inproj(h, w_in[l].astype(bf16), 2048, 512)
        proj3 = proj.reshape(b, s, N_MAIN)

        ga = _branch_a(proj3, sink[l].astype(jnp.float32), 512).reshape(b * s, WIDTH_A)

        bs_bcast = jnp.broadcast_to(b_spatial[l].astype(jnp.float32)[:, :, None],
                                    (N_GROUPS_B, CHUNK, CHUNK))
        gb = _branch_b(proj, ln_v_gain[l], ln_v_bias[l], w_spatial[l].astype(bf16), bs_bcast, 512)

        mem_h = _rmsnorm_bf16(mem2d, mem_norm_gain[l], 256)
        kv = _mem_kv(mem_h, w_kv_mem[l].astype(bf16), 512)
        gc = _branch_c(proj3, kv.reshape(b, MEM_LEN, 2 * WIDTH_C), 512).reshape(b * s, WIDTH_C)

        x2d = _merge(x2d, ga, gb, gc, gates, w_br_a[l].astype(bf16), w_br_b[l].astype(bf16),
                     w_br_c[l].astype(bf16), w_out[l].astype(bf16), final_gain, 256,
                     final_norm=(l == depth - 1))
    return x2d.reshape(b, s, d)
```

```python
import functools
import math

import jax
import jax.numpy as jnp
import numpy as np
from jax import lax
from jax.experimental import pallas as pl
from jax.experimental.pallas import tpu as pltpu

D_MODEL = 2048
MEM_LEN = 256
EPS = 1e-6
HEAD_DIM_A = 128
WIDTH_A = 1536
N_Q_HEADS_A = 12
N_KV_HEADS_A = 4
REP_A = N_Q_HEADS_A // N_KV_HEADS_A
KV_WIDTH_A = 512
BLOCK = 128
WIDTH_B = 1536
CHUNK = 128
N_GROUPS_B = 12
N_HEADS_C = 4
WIDTH_C = 1024
HEAD_DIM_C = 256
NEG_INF = -1e30
N_GATES = 3 * D_MODEL

SQRT_HALF = math.sqrt(0.5)
LOG2_E = math.log2(math.e)
VMEM_LIMIT = 56 * 1024 * 1024

COL_BLOCK = 512
TRIPLE = 3 * COL_BLOCK
MAIN_SRC_BLOCKS = (0, 1, 2,
                   3, 4, 17,
                   5, 6, 7,
                   14, 15, 16,
                   8, 9, 10,
                   11, 12, 13,
                   18, 19, 20)
MAIN_PATTERNS = (("plain",) * 3, ("silu",) * 3, ("gelu",) * 3, ("plain", "silu", "silu"))
MAIN_STEP_PATTERN = (0, 0, 1, 1, 2, 2, 3)
N_MAIN = len(MAIN_SRC_BLOCKS) * COL_BLOCK
GATE_SRC_BLOCKS = tuple(range(21, 33))
GATE_PATTERNS = (("sigmoid",) * 3,)
GATE_STEP_PATTERN = (0, 0, 0, 0)
OUT_Q_A_T, OUT_Z_A_T, OUT_Z_B_T, OUT_U_B_T, OUT_V_B_T = 0, 2, 3, 4, 5
OUT_K_A, OUT_V_A = 3, 4
OUT_Q_C = (5, 18)
OUT_Z_C = (19, 20)


def _alibi_slopes(n):
    def pow2_slopes(m):
        start = 2.0 ** (-8.0 / m)
        return [start ** (i + 1) for i in range(m)]
    if math.log2(n).is_integer():
        s = pow2_slopes(n)
    else:
        c = 2 ** int(math.floor(math.log2(n)))
        s = pow2_slopes(c) + pow2_slopes(2 * c)[0::2][: n - c]
    return [float(v) for v in np.asarray(s, dtype=np.float32)]


ALIBI_SLOPES = _alibi_slopes(N_Q_HEADS_A)


def _sigmoid(x):
    return 0.5 * (1.0 + jnp.tanh(0.5 * x))


def _activation(x, kind):
    if kind == "plain":
        return x
    if kind == "silu":
        return x * _sigmoid(x)
    if kind == "gelu":
        return 0.5 * x * (1.0 + lax.erf(x * SQRT_HALF))
    if kind == "sigmoid":
        return _sigmoid(x)
    raise ValueError(kind)


def _dot(a, b):
    return jnp.dot(a, b, preferred_element_type=jnp.float32)


def _dot_nt(a, b):
    return lax.dot_general(a, b, (((1,), (1,)), ((), ())), preferred_element_type=jnp.float32)


def _rmsnorm_kernel(x_ref, g_ref, o_ref):
    x = x_ref[...]
    ms = jnp.mean(x * x, axis=-1, keepdims=True)
    o_ref[...] = ((x * lax.rsqrt(ms + EPS)) * g_ref[...]).astype(o_ref.dtype)


def _rmsnorm_bf16(x2d, gain, tm):
    t, d = x2d.shape
    return pl.pallas_call(
        _rmsnorm_kernel,
        grid=(t // tm,),
        in_specs=[pl.BlockSpec((tm, d), lambda i: (i, 0)),
                  pl.BlockSpec((1, d), lambda i: (0, 0))],
        out_specs=pl.BlockSpec((tm, d), lambda i: (i, 0)),
        out_shape=jax.ShapeDtypeStruct((t, d), jnp.bfloat16),
        compiler_params=pltpu.CompilerParams(dimension_semantics=("arbitrary",),
                                             vmem_limit_bytes=VMEM_LIMIT),
        name="rmsnorm_bf16",
    )(x2d, gain.reshape(1, d))


def _inproj_kernel(src_ref, pat_ref, h_ref, w0_ref, w1_ref, w2_ref, o_ref, *, patterns, chunk):
    j = pl.program_id(1)
    tm = h_ref.shape[0]
    w_refs = (w0_ref, w1_ref, w2_ref)
    for p, pattern in enumerate(patterns):
        @pl.when(pat_ref[j] == p)
        def _(pattern=pattern):
            for c in range(tm // chunk):
                rows = slice(c * chunk, (c + 1) * chunk)
                hc = h_ref[rows, :]
                for s in range(3):
                    acc = _dot(hc, w_refs[s][...])
                    o_ref[rows, s * COL_BLOCK:(s + 1) * COL_BLOCK] = _activation(
                        acc, pattern[s]).astype(o_ref.dtype)


def _inproj(h, w_in_bf16, src_blocks, step_pattern, patterns, tm, chunk, name):
    t, d = h.shape
    n_steps = len(step_pattern)
    assert len(src_blocks) == 3 * n_steps

    def w_spec(s):
        return pl.BlockSpec((d, COL_BLOCK), lambda i, j, src_ref, pat_ref: (0, src_ref[3 * j + s]))

    return pl.pallas_call(
        functools.partial(_inproj_kernel, patterns=patterns, chunk=chunk),
        grid_spec=pltpu.PrefetchScalarGridSpec(
            num_scalar_prefetch=2,
            grid=(t // tm, n_steps),
            in_specs=[pl.BlockSpec((tm, d), lambda i, j, src_ref, pat_ref: (i, 0)),
                      w_spec(0), w_spec(1), w_spec(2)],
            out_specs=pl.BlockSpec((tm, TRIPLE), lambda i, j, src_ref, pat_ref: (i, j)),
        ),
        out_shape=jax.ShapeDtypeStruct((t, n_steps * TRIPLE), jnp.bfloat16),
        compiler_params=pltpu.CompilerParams(dimension_semantics=("arbitrary", "arbitrary"),
                                             vmem_limit_bytes=VMEM_LIMIT),
        name=name,
    )(jnp.asarray(src_blocks, jnp.int32), jnp.asarray(step_pattern, jnp.int32),
      h, w_in_bf16, w_in_bf16, w_in_bf16)


def _attn_kernel(sink_ref, q_ref, kp_ref, kc_ref, kn_ref, vp_ref, vc_ref, vn_ref, z_ref, o_ref,
                 kbuf, vbuf, bias_ref):
    tq = q_ref.shape[0]
    nblk = tq // BLOCK
    bi = pl.program_id(0)
    n = pl.program_id(1)
    nt = pl.num_programs(1)
    c1 = LOG2_E / math.sqrt(HEAD_DIM_A)

    @pl.when((bi == 0) & (n == 0))
    def _():
        row = lax.broadcasted_iota(jnp.int32, (BLOCK, 3 * BLOCK), 0)
        col = lax.broadcasted_iota(jnp.int32, (BLOCK, 3 * BLOCK), 1)
        dist = jnp.abs(row - (col - BLOCK))
        valid = dist <= BLOCK
        distf = dist.astype(jnp.float32)
        for h in range(N_Q_HEADS_A):
            hk, r = divmod(h, REP_A)
            bias_ref[hk, r * BLOCK:(r + 1) * BLOCK, :] = jnp.where(
                valid, (-ALIBI_SLOPES[h] * LOG2_E) * distf, NEG_INF)
        for hk in range(N_KV_HEADS_A):
            vbuf[:, (2 * hk + 1) * HEAD_DIM_A:(2 * hk + 2) * HEAD_DIM_A] = jnp.ones(
                (tq + 2 * BLOCK, HEAD_DIM_A), vbuf.dtype)

    kbuf[0:BLOCK, :] = kp_ref[...]
    kbuf[BLOCK:BLOCK + tq, :] = kc_ref[...]
    kbuf[BLOCK + tq:, :] = kn_ref[...]
    for hk in range(N_KV_HEADS_A):
        src = slice(hk * HEAD_DIM_A, (hk + 1) * HEAD_DIM_A)
        dst = slice(2 * hk * HEAD_DIM_A, (2 * hk + 1) * HEAD_DIM_A)
        vbuf[0:BLOCK, dst] = vp_ref[:, src]
        vbuf[BLOCK:BLOCK + tq, dst] = vc_ref[:, src]
        vbuf[BLOCK + tq:, dst] = vn_ref[:, src]

    edge_p = jnp.where(n == 0, NEG_INF, 0.0)
    edge_n = jnp.where(n == nt - 1, NEG_INF, 0.0)

    units = [(blk, hk) for blk in range(nblk) for hk in range(N_KV_HEADS_A)]

    def scores(unit):
        blk, hk = unit
        rows = slice(blk * BLOCK, (blk + 1) * BLOCK)
        q = jnp.concatenate(
            [q_ref[rows, (hk * REP_A + r) * HEAD_DIM_A:(hk * REP_A + r + 1) * HEAD_DIM_A]
             for r in range(REP_A)], axis=0)
        k = kbuf[blk * BLOCK:(blk + 3) * BLOCK, hk * HEAD_DIM_A:(hk + 1) * HEAD_DIM_A]
        return _dot_nt(q, k)

    s_next = scores(units[0])
    for i, (blk, hk) in enumerate(units):
        s = s_next
        if i + 1 < len(units):
            s_next = scores(units[i + 1])
        rows = slice(blk * BLOCK, (blk + 1) * BLOCK)
        krows = slice(blk * BLOCK, (blk + 3) * BLOCK)
        t = s * c1 + bias_ref[hk]
        t_p, t_c, t_n = t[:, 0:BLOCK], t[:, BLOCK:2 * BLOCK], t[:, 2 * BLOCK:]
        if blk == 0:
            t_p = t_p + edge_p
        if blk == nblk - 1:
            t_n = t_n + edge_n
        sink2 = jnp.concatenate(
            [jnp.full((BLOCK, BLOCK), sink_ref[hk * REP_A + r] * LOG2_E, jnp.float32)
             for r in range(REP_A)], axis=0)
        m = jnp.max(jnp.maximum(jnp.maximum(t_p, t_c), t_n), axis=-1, keepdims=True)
        m = jnp.maximum(jnp.broadcast_to(m, (REP_A * BLOCK, BLOCK)), sink2)
        p = jnp.concatenate([jnp.exp2(t_p - m), jnp.exp2(t_c - m), jnp.exp2(t_n - m)],
                            axis=1).astype(jnp.bfloat16)
        ov = _dot(p, vbuf[krows, 2 * hk * HEAD_DIM_A:(2 * hk + 2) * HEAD_DIM_A])
        denom = ov[:, HEAD_DIM_A:] + jnp.exp2(sink2 - m)
        o = ov[:, :HEAD_DIM_A] * (1.0 / denom)
        for r in range(REP_A):
            hsl = slice((hk * REP_A + r) * HEAD_DIM_A, (hk * REP_A + r + 1) * HEAD_DIM_A)
            z = z_ref[rows, hsl].astype(jnp.float32)
            o_ref[rows, hsl] = (o[r * BLOCK:(r + 1) * BLOCK] * z).astype(o_ref.dtype)


def _branch_a(proj3, sink, tq):
    b, s, _ = proj3.shape
    nb = s // BLOCK
    per = tq // BLOCK

    def tile(width, idx):
        return pl.BlockSpec((None, tq, width), lambda bi, n, sink_ref: (bi, n, idx))

    def halo(idx, after):
        def imap(bi, n, sink_ref):
            blk = (n + 1) * per if after else n * per - 1
            return (bi, jnp.clip(blk, 0, nb - 1), idx)
        return pl.BlockSpec((None, BLOCK, KV_WIDTH_A), imap)

    in_specs = [
        tile(WIDTH_A, OUT_Q_A_T),
        halo(OUT_K_A, False), tile(KV_WIDTH_A, OUT_K_A), halo(OUT_K_A, True),
        halo(OUT_V_A, False), tile(KV_WIDTH_A, OUT_V_A), halo(OUT_V_A, True),
        tile(WIDTH_A, OUT_Z_A_T),
    ]
    return pl.pallas_call(
        _attn_kernel,
        grid_spec=pltpu.PrefetchScalarGridSpec(
            num_scalar_prefetch=1,
            grid=(b, s // tq),
            in_specs=in_specs,
            out_specs=pl.BlockSpec((None, tq, WIDTH_A), lambda bi, n, sink_ref: (bi, n, 0)),
            scratch_shapes=[pltpu.VMEM((tq + 2 * BLOCK, KV_WIDTH_A), jnp.bfloat16),
                            pltpu.VMEM((tq + 2 * BLOCK, 2 * KV_WIDTH_A), jnp.bfloat16),
                            pltpu.VMEM((N_KV_HEADS_A, REP_A * BLOCK, 3 * BLOCK), jnp.float32)],
        ),
        out_shape=jax.ShapeDtypeStruct((b, s, WIDTH_A), jnp.bfloat16),
        compiler_params=pltpu.CompilerParams(dimension_semantics=("arbitrary", "arbitrary"),
                                             vmem_limit_bytes=VMEM_LIMIT),
        name="branch_a_attn",
    )(sink, *([proj3] * 8))


def _branch_b_kernel(u_ref, v_ref, z_ref, lng_ref, lnb_ref, ws_ref, bs_ref, o_ref):
    tm = o_ref.shape[0]
    nc = tm // CHUNK
    gv = v_ref[...].astype(jnp.float32)
    mu = jnp.mean(gv, axis=-1, keepdims=True)
    xc = gv - mu
    var = jnp.mean(xc * xc, axis=-1, keepdims=True)
    vn = (xc * lax.rsqrt(var + EPS) * lng_ref[...] + lnb_ref[...]).astype(jnp.bfloat16)
    for g in range(N_GROUPS_B):
        gsl = slice(g * CHUNK, (g + 1) * CHUNK)
        rhs = jnp.concatenate([vn[c * CHUNK:(c + 1) * CHUNK, gsl] for c in range(nc)], axis=1)
        sg = _dot(ws_ref[g], rhs)
        bias = bs_ref[g]
        for c in range(nc):
            rows = slice(c * CHUNK, (c + 1) * CHUNK)
            s_c = sg[:, c * CHUNK:(c + 1) * CHUNK] + bias
            u = u_ref[rows, gsl].astype(jnp.float32)
            z = z_ref[rows, gsl].astype(jnp.float32)
            o_ref[rows, gsl] = (u * s_c * z).astype(o_ref.dtype)


def _branch_b(proj, ln_gain, ln_bias, w_spatial_bf16, b_spatial_bcast, tm):
    t = proj.shape[0]

    def blk(idx):
        return pl.BlockSpec((tm, WIDTH_B), lambda i: (i, idx))

    in_specs = [blk(OUT_U_B_T), blk(OUT_V_B_T), blk(OUT_Z_B_T),
                pl.BlockSpec((1, WIDTH_B), lambda i: (0, 0)),
                pl.BlockSpec((1, WIDTH_B), lambda i: (0, 0)),
                pl.BlockSpec((N_GROUPS_B, CHUNK, CHUNK), lambda i: (0, 0, 0)),
                pl.BlockSpec((N_GROUPS_B, CHUNK, CHUNK), lambda i: (0, 0, 0))]
    return pl.pallas_call(
        _branch_b_kernel,
        grid=(t // tm,),
        in_specs=in_specs,
        out_specs=pl.BlockSpec((tm, WIDTH_B), lambda i: (i, 0)),
        out_shape=jax.ShapeDtypeStruct((t, WIDTH_B), jnp.bfloat16),
        compiler_params=pltpu.CompilerParams(dimension_semantics=("arbitrary",),
                                             vmem_limit_bytes=VMEM_LIMIT),
        name="branch_b_gmlp",
    )(proj, proj, proj, ln_gain.reshape(1, WIDTH_B), ln_bias.reshape(1, WIDTH_B),
      w_spatial_bf16, b_spatial_bcast)


def _matmul_kernel(a_ref, b_ref, o_ref):
    o_ref[...] = _dot(a_ref[...], b_ref[...]).astype(o_ref.dtype)


def _mem_kv(mem_h, w_kv_bf16, tn):
    m, d = mem_h.shape
    n = w_kv_bf16.shape[1]
    return pl.pallas_call(
        _matmul_kernel,
        grid=(n // tn,),
        in_specs=[pl.BlockSpec((m, d), lambda j: (0, 0)),
                  pl.BlockSpec((d, tn), lambda j: (0, j))],
        out_specs=pl.BlockSpec((m, tn), lambda j: (0, j)),
        out_shape=jax.ShapeDtypeStruct((m, n), jnp.bfloat16),
        compiler_params=pltpu.CompilerParams(dimension_semantics=("arbitrary",),
                                             vmem_limit_bytes=VMEM_LIMIT),
        name="mem_kv",
    )(mem_h, w_kv_bf16)


def _branch_c_kernel(q0, q1, z0, z1, kv_ref, o_ref):
    scale = 1.0 / math.sqrt(HEAD_DIM_C)
    q_refs = (q0, q1)
    z_refs = (z0, z1)
    for h in range(N_HEADS_C):
        cb, off = divmod(h * HEAD_DIM_C, COL_BLOCK)
        q = q_refs[cb][:, off:off + HEAD_DIM_C]
        k = kv_ref[:, h * HEAD_DIM_C:(h + 1) * HEAD_DIM_C]
        v = kv_ref[:, WIDTH_C + h * HEAD_DIM_C:WIDTH_C + (h + 1) * HEAD_DIM_C]
        s = _dot_nt(q, k) * scale
        m = jnp.max(s, axis=-1, keepdims=True)
        p = jnp.exp(s - m)
        denom = jnp.sum(p, axis=-1, keepdims=True)
        o = _dot(p.astype(jnp.bfloat16), v) / denom
        z = z_refs[cb][:, off:off + HEAD_DIM_C].astype(jnp.float32)
        o_ref[:, h * HEAD_DIM_C:(h + 1) * HEAD_DIM_C] = (o * z).astype(o_ref.dtype)


def _branch_c(proj3, kv3, tm):
    b, s, _ = proj3.shape

    def blk(cb):
        return pl.BlockSpec((None, tm, COL_BLOCK), lambda bi, i: (bi, i, cb))

    in_specs = [blk(OUT_Q_C[0]), blk(OUT_Q_C[1]), blk(OUT_Z_C[0]), blk(OUT_Z_C[1]),
                pl.BlockSpec((None, MEM_LEN, 2 * WIDTH_C), lambda bi, i: (bi, 0, 0))]
    return pl.pallas_call(
        _branch_c_kernel,
        grid=(b, s // tm),
        in_specs=in_specs,
        out_specs=pl.BlockSpec((None, tm, WIDTH_C), lambda bi, i: (bi, i, 0)),
        out_shape=jax.ShapeDtypeStruct((b, s, WIDTH_C), jnp.bfloat16),
        compiler_params=pltpu.CompilerParams(dimension_semantics=("arbitrary", "arbitrary"),
                                             vmem_limit_bytes=VMEM_LIMIT),
        name="branch_c_xattn",
    )(proj3, proj3, proj3, proj3, kv3)


def _merge_kernel(x_ref, ga_ref, gb_ref, gc_ref, gates_ref, wa_ref, wb_ref, wc_ref, wo_ref,
                  fg_ref, o_ref, *, final_norm):
    d = D_MODEL
    merged = _dot(ga_ref[...], wa_ref[...]) * gates_ref[:, 0:d].astype(jnp.float32)
    merged = merged + _dot(gb_ref[...], wb_ref[...]) * gates_ref[:, d:2 * d].astype(jnp.float32)
    merged = merged + _dot(gc_ref[...], wc_ref[...]) * gates_ref[:, 2 * d:3 * d].astype(jnp.float32)
    y = x_ref[...] + _dot(merged.astype(jnp.bfloat16), wo_ref[...])
    if final_norm:
        ms = jnp.mean(y * y, axis=-1, keepdims=True)
        y = (y * lax.rsqrt(ms + EPS)) * fg_ref[...]
    o_ref[...] = y


def _merge(x2d, ga, gb, gc, gates, wa, wb, wc, wo, final_gain, tm, final_norm):
    t, d = x2d.shape

    def rows(width):
        return pl.BlockSpec((tm, width), lambda i: (i, 0))

    def resident(shape):
        return pl.BlockSpec(shape, lambda i: (0, 0), pipeline_mode=pl.Buffered(1))

    return pl.pallas_call(
        functools.partial(_merge_kernel, final_norm=final_norm),
        grid=(t // tm,),
        in_specs=[rows(d), rows(WIDTH_A), rows(WIDTH_B), rows(WIDTH_C), rows(N_GATES),
                  resident((WIDTH_A, d)), resident((WIDTH_B, d)), resident((WIDTH_C, d)),
                  resident((d, d)), resident((1, d))],
        out_specs=rows(d),
        out_shape=jax.ShapeDtypeStruct((t, d), jnp.float32),
        compiler_params=pltpu.CompilerParams(dimension_semantics=("arbitrary",),
                                             vmem_limit_bytes=VMEM_LIMIT),
        name="merge_out",
    )(x2d, ga, gb, gc, gates, wa, wb, wc, wo, final_gain.reshape(1, d))


def kernel(x, mem, norm_gain, mem_norm_gain, w_in, sink, ln_v_gain, ln_v_bias, w_spatial, b_spatial,
           w_kv_mem, w_br_a, w_br_b, w_br_c, w_out, final_gain):
    b, s, d = x.shape
    depth = w_in.shape[0]
    bf16 = jnp.bfloat16
    x2d = x.reshape(b * s, d)
    mem2d = mem.reshape(b * MEM_LEN, d)
    for l in range(depth):
        h = _rmsnorm_bf16(x2d, norm_gain[l], 512)
        w_in_bf16 = w_in[l].astype(bf16)
        proj = _inproj(h, w_in_bf16, MAIN_SRC_BLOCKS, MAIN_STEP_PATTERN, MAIN_PATTERNS,
                       2048, 512, "inproj_main")
        gates = _inproj(h, w_in_bf16, GATE_SRC_BLOCKS, GATE_STEP_PATTERN, GATE_PATTERNS,
                        2048, 512, "inproj_gates")
        proj3 = proj.reshape(b, s, N_MAIN)

        ga = _branch_a(proj3, sink[l].astype(jnp.float32), 512).reshape(b * s, WIDTH_A)

        bs_bcast = jnp.broadcast_to(b_spatial[l].astype(jnp.float32)[:, :, None],
                                    (N_GROUPS_B, CHUNK, CHUNK))
        gb = _branch_b(proj, ln_v_gain[l], ln_v_bias[l], w_spatial[l].astype(bf16), bs_bcast, 512)

        mem_h = _rmsnorm_bf16(mem2d, mem_norm_gain[l], 256)
        kv = _mem_kv(mem_h, w_kv_mem[l].astype(bf16), 512)
        gc = _branch_c(proj3, kv.reshape(b, MEM_LEN, 2 * WIDTH_C), 512).reshape(b * s, WIDTH_C)

        x2d = _merge(x2d, ga, gb, gc, gates, w_br_a[l].astype(bf16), w_br_b[l].astype(bf16),
                     w_br_c[l].astype(bf16), w_out[l].astype(bf16), final_gain, 256,
                     final_norm=(l == depth - 1))
    return x2d.reshape(b, s, d)
```

```python
import functools
import math

import jax
import jax.numpy as jnp
import numpy as np
from jax import lax
from jax.experimental import pallas as pl
from jax.experimental.pallas import tpu as pltpu

D_MODEL = 2048
MEM_LEN = 256
EPS = 1e-6
HEAD_DIM_A = 128
WIDTH_A = 1536
N_Q_HEADS_A = 12
N_KV_HEADS_A = 4
REP_A = N_Q_HEADS_A // N_KV_HEADS_A
KV_WIDTH_A = 512
BLOCK = 128
WIDTH_B = 1536
CHUNK = 128
N_GROUPS_B = 12
N_HEADS_C = 4
WIDTH_C = 1024
HEAD_DIM_C = 256
NEG_INF = -1e30
N_GATES = 3 * D_MODEL

SQRT_HALF = math.sqrt(0.5)
LOG2_E = math.log2(math.e)
VMEM_LIMIT = 56 * 1024 * 1024

COL_BLOCK = 512
TRIPLE = 3 * COL_BLOCK
MAIN_SRC_BLOCKS = (0, 1, 2,
                   3, 4, 17,
                   5, 6, 7,
                   14, 15, 16,
                   8, 9, 10,
                   11, 12, 13,
                   18, 19, 20)
MAIN_PATTERNS = (("plain",) * 3, ("silu",) * 3, ("gelu",) * 3, ("plain", "silu", "silu"))
MAIN_STEP_PATTERN = (0, 0, 1, 1, 2, 2, 3)
N_MAIN = len(MAIN_SRC_BLOCKS) * COL_BLOCK
GATE_SRC_BLOCKS = tuple(range(21, 33))
GATE_PATTERNS = (("sigmoid",) * 3,)
GATE_STEP_PATTERN = (0, 0, 0, 0)
OUT_Q_A_T, OUT_Z_A_T, OUT_Z_B_T, OUT_U_B_T, OUT_V_B_T = 0, 2, 3, 4, 5
OUT_K_A, OUT_V_A = 3, 4
OUT_Q_C = (5, 18)
OUT_Z_C = (19, 20)


def _alibi_slopes(n):
    def pow2_slopes(m):
        start = 2.0 ** (-8.0 / m)
        return [start ** (i + 1) for i in range(m)]
    if math.log2(n).is_integer():
        s = pow2_slopes(n)
    else:
        c = 2 ** int(math.floor(math.log2(n)))
        s = pow2_slopes(c) + pow2_slopes(2 * c)[0::2][: n - c]
    return [float(v) for v in np.asarray(s, dtype=np.float32)]


ALIBI_SLOPES = _alibi_slopes(N_Q_HEADS_A)


def _sigmoid(x):
    return 0.5 * (1.0 + jnp.tanh(0.5 * x))


def _activation(x, kind):
    if kind == "plain":
        return x
    if kind == "silu":
        return x * _sigmoid(x)
    if kind == "gelu":
        return 0.5 * x * (1.0 + lax.erf(x * SQRT_HALF))
    if kind == "sigmoid":
        return _sigmoid(x)
    raise ValueError(kind)


def _dot(a, b):
    return jnp.dot(a, b, preferred_element_type=jnp.float32)


def _dot_nt(a, b):
    return lax.dot_general(a, b, (((1,), (1,)), ((), ())), preferred_element_type=jnp.float32)


def _rmsnorm_kernel(x_ref, g_ref, o_ref):
    x = x_ref[...]
    ms = jnp.mean(x * x, axis=-1, keepdims=True)
    o_ref[...] = ((x * lax.rsqrt(ms + EPS)) * g_ref[...]).astype(o_ref.dtype)


def _rmsnorm_bf16(x2d, gain, tm):
    t, d = x2d.shape
    return pl.pallas_call(
        _rmsnorm_kernel,
        grid=(t // tm,),
        in_specs=[pl.BlockSpec((tm, d), lambda i: (i, 0)),
                  pl.BlockSpec((1, d), lambda i: (0, 0))],
        out_specs=pl.BlockSpec((tm, d), lambda i: (i, 0)),
        out_shape=jax.ShapeDtypeStruct((t, d), jnp.bfloat16),
        compiler_params=pltpu.CompilerParams(dimension_semantics=("arbitrary",),
                                             vmem_limit_bytes=VMEM_LIMIT),
        name="rmsnorm_bf16",
    )(x2d, gain.reshape(1, d))


def _inproj_kernel(src_ref, pat_ref, h_ref, w0_ref, w1_ref, w2_ref, o_ref, *, patterns, chunk):
    j = pl.program_id(1)
    tm = h_ref.shape[0]
    w_refs = (w0_ref, w1_ref, w2_ref)
    for p, pattern in enumerate(patterns):
        @pl.when(pat_ref[j] == p)
        def _(pattern=pattern):
            for c in range(tm // chunk):
                rows = slice(c * chunk, (c + 1) * chunk)
                hc = h_ref[rows, :]
                for s in range(3):
                    acc = _dot(hc, w_refs[s][...])
                    o_ref[rows, s * COL_BLOCK:(s + 1) * COL_BLOCK] = _activation(
                        acc, pattern[s]).astype(o_ref.dtype)


def _inproj(h, w_in_bf16, src_blocks, step_pattern, patterns, tm, chunk, name):
    t, d = h.shape
    n_steps = len(step_pattern)
    assert len(src_blocks) == 3 * n_steps

    def w_spec(s):
        return pl.BlockSpec((d, COL_BLOCK), lambda i, j, src_ref, pat_ref: (0, src_ref[3 * j + s]))

    return pl.pallas_call(
        functools.partial(_inproj_kernel, patterns=patterns, chunk=chunk),
        grid_spec=pltpu.PrefetchScalarGridSpec(
            num_scalar_prefetch=2,
            grid=(t // tm, n_steps),
            in_specs=[pl.BlockSpec((tm, d), lambda i, j, src_ref, pat_ref: (i, 0)),
                      w_spec(0), w_spec(1), w_spec(2)],
            out_specs=pl.BlockSpec((tm, TRIPLE), lambda i, j, src_ref, pat_ref: (i, j)),
        ),
        out_shape=jax.ShapeDtypeStruct((t, n_steps * TRIPLE), jnp.bfloat16),
        compiler_params=pltpu.CompilerParams(dimension_semantics=("arbitrary", "arbitrary"),
                                             vmem_limit_bytes=VMEM_LIMIT),
        name=name,
    )(jnp.asarray(src_blocks, jnp.int32), jnp.asarray(step_pattern, jnp.int32),
      h, w_in_bf16, w_in_bf16, w_in_bf16)


def _attn_kernel(sink_ref, q_ref, kp_ref, kc_ref, kn_ref, vp_ref, vc_ref, vn_ref, z_ref, o_ref,
                 kbuf, vbuf, bias_ref):
    tq = q_ref.shape[0]
    nblk = tq // BLOCK
    bi = pl.program_id(0)
    n = pl.program_id(1)
    nt = pl.num_programs(1)
    c1 = LOG2_E / math.sqrt(HEAD_DIM_A)

    @pl.when((bi == 0) & (n == 0))
    def _():
        row = lax.broadcasted_iota(jnp.int32, (BLOCK, 3 * BLOCK), 0)
        col = lax.broadcasted_iota(jnp.int32, (BLOCK, 3 * BLOCK), 1)
        dist = jnp.abs(row - (col - BLOCK))
        valid = dist <= BLOCK
        distf = dist.astype(jnp.float32)
        for h in range(N_Q_HEADS_A):
            hk, r = divmod(h, REP_A)
            bias_ref[hk, r * BLOCK:(r + 1) * BLOCK, :] = jnp.where(
                valid, (-ALIBI_SLOPES[h] * LOG2_E) * distf, NEG_INF)
        for hk in range(N_KV_HEADS_A):
            vbuf[:, (2 * hk + 1) * HEAD_DIM_A:(2 * hk + 2) * HEAD_DIM_A] = jnp.ones(
                (tq + 2 * BLOCK, HEAD_DIM_A), vbuf.dtype)

    kbuf[0:BLOCK, :] = kp_ref[...]
    kbuf[BLOCK:BLOCK + tq, :] = kc_ref[...]
    kbuf[BLOCK + tq:, :] = kn_ref[...]
    for hk in range(N_KV_HEADS_A):
        src = slice(hk * HEAD_DIM_A, (hk + 1) * HEAD_DIM_A)
        dst = slice(2 * hk * HEAD_DIM_A, (2 * hk + 1) * HEAD_DIM_A)
        vbuf[0:BLOCK, dst] = vp_ref[:, src]
        vbuf[BLOCK:BLOCK + tq, dst] = vc_ref[:, src]
        vbuf[BLOCK + tq:, dst] = vn_ref[:, src]

    edge_p = jnp.where(n == 0, NEG_INF, 0.0)
    edge_n = jnp.where(n == nt - 1, NEG_INF, 0.0)

    units = [(blk, hk) for blk in range(nblk) for hk in range(N_KV_HEADS_A)]

    def scores(unit):
        blk, hk = unit
        rows = slice(blk * BLOCK, (blk + 1) * BLOCK)
        q = jnp.concatenate(
            [q_ref[rows, (hk * REP_A + r) * HEAD_DIM_A:(hk * REP_A + r + 1) * HEAD_DIM_A]
             for r in range(REP_A)], axis=0)
        k = kbuf[blk * BLOCK:(blk + 3) * BLOCK, hk * HEAD_DIM_A:(hk + 1) * HEAD_DIM_A]
        return _dot_nt(q, k)

    s_next = scores(units[0])
    for i, (blk, hk) in enumerate(units):
        s = s_next
        if i + 1 < len(units):
            s_next = scores(units[i + 1])
        rows = slice(blk * BLOCK, (blk + 1) * BLOCK)
        krows = slice(blk * BLOCK, (blk + 3) * BLOCK)
        t = s * c1 + bias_ref[hk]
        t_p, t_c, t_n = t[:, 0:BLOCK], t[:, BLOCK:2 * BLOCK], t[:, 2 * BLOCK:]
        if blk == 0:
            t_p = t_p + edge_p
        if blk == nblk - 1:
            t_n = t_n + edge_n
        sink2 = jnp.concatenate(
            [jnp.full((BLOCK, BLOCK), sink_ref[hk * REP_A + r] * LOG2_E, jnp.float32)
             for r in range(REP_A)], axis=0)
        m = jnp.max(jnp.maximum(jnp.maximum(t_p, t_c), t_n), axis=-1, keepdims=True)
        m = jnp.maximum(jnp.broadcast_to(m, (REP_A * BLOCK, BLOCK)), sink2)
        p = jnp.concatenate([jnp.exp2(t_p - m), jnp.exp2(t_c - m), jnp.exp2(t_n - m)],
                            axis=1).astype(jnp.bfloat16)
        ov = _dot(p, vbuf[krows, 2 * hk * HEAD_DIM_A:(2 * hk + 2) * HEAD_DIM_A])
        denom = ov[:, HEAD_DIM_A:] + jnp.exp2(sink2 - m)
        o = ov[:, :HEAD_DIM_A] * (1.0 / denom)
        for r in range(REP_A):
            hsl = slice((hk * REP_A + r) * HEAD_DIM_A, (hk * REP_A + r + 1) * HEAD_DIM_A)
            z = z_ref[rows, hsl].astype(jnp.float32)
            o_ref[rows, hsl] = (o[r * BLOCK:(r + 1) * BLOCK] * z).astype(o_ref.dtype)


def _branch_a(proj3, sink, tq):
    b, s, _ = proj3.shape
    nb = s // BLOCK
    per = tq // BLOCK

    def tile(width, idx):
        return pl.BlockSpec((None, tq, width), lambda bi, n, sink_ref: (bi, n, idx))

    def halo(idx, after):
        def imap(bi, n, sink_ref):
            blk = (n + 1) * per if after else n * per - 1
            return (bi, jnp.clip(blk, 0, nb - 1), idx)
        return pl.BlockSpec((None, BLOCK, KV_WIDTH_A), imap)

    in_specs = [
        tile(WIDTH_A, OUT_Q_A_T),
        halo(OUT_K_A, False), tile(KV_WIDTH_A, OUT_K_A), halo(OUT_K_A, True),
        halo(OUT_V_A, False), tile(KV_WIDTH_A, OUT_V_A), halo(OUT_V_A, True),
        tile(WIDTH_A, OUT_Z_A_T),
    ]
    return pl.pallas_call(
        _attn_kernel,
        grid_spec=pltpu.PrefetchScalarGridSpec(
            num_scalar_prefetch=1,
            grid=(b, s // tq),
            in_specs=in_specs,
            out_specs=pl.BlockSpec((None, tq, WIDTH_A), lambda bi, n, sink_ref: (bi, n, 0)),
            scratch_shapes=[pltpu.VMEM((tq + 2 * BLOCK, KV_WIDTH_A), jnp.bfloat16),
                            pltpu.VMEM((tq + 2 * BLOCK, 2 * KV_WIDTH_A), jnp.bfloat16),
                            pltpu.VMEM((N_KV_HEADS_A, REP_A * BLOCK, 3 * BLOCK), jnp.float32)],
        ),
        out_shape=jax.ShapeDtypeStruct((b, s, WIDTH_A), jnp.bfloat16),
        compiler_params=pltpu.CompilerParams(dimension_semantics=("arbitrary", "arbitrary"),
                                             vmem_limit_bytes=VMEM_LIMIT),
        name="branch_a_attn",
    )(sink, *([proj3] * 8))


def _gmlp_tile(u_ref, v_ref, z_ref, lng_ref, lnb_ref, ws_ref, bs_ref, o_ref):
    tm = o_ref.shape[0]
    nc = tm // CHUNK
    gv = v_ref[...].astype(jnp.float32)
    mu = jnp.mean(gv, axis=-1, keepdims=True)
    xc = gv - mu
    var = jnp.mean(xc * xc, axis=-1, keepdims=True)
    vn = (xc * lax.rsqrt(var + EPS) * lng_ref[...] + lnb_ref[...]).astype(jnp.bfloat16)
    for g in range(N_GROUPS_B):
        gsl = slice(g * CHUNK, (g + 1) * CHUNK)
        rhs = jnp.concatenate([vn[c * CHUNK:(c + 1) * CHUNK, gsl] for c in range(nc)], axis=1)
        sg = _dot(ws_ref[g], rhs)
        bias = bs_ref[g]
        for c in range(nc):
            rows = slice(c * CHUNK, (c + 1) * CHUNK)
            s_c = sg[:, c * CHUNK:(c + 1) * CHUNK] + bias
            u = u_ref[rows, gsl].astype(jnp.float32)
            z = z_ref[rows, gsl].astype(jnp.float32)
            o_ref[rows, gsl] = (u * s_c * z).astype(o_ref.dtype)


def _matmul_kernel(a_ref, b_ref, o_ref):
    o_ref[...] = _dot(a_ref[...], b_ref[...]).astype(o_ref.dtype)


def _mem_kv(mem_h, w_kv_bf16, tn):
    m, d = mem_h.shape
    n = w_kv_bf16.shape[1]
    return pl.pallas_call(
        _matmul_kernel,
        grid=(n // tn,),
        in_specs=[pl.BlockSpec((m, d), lambda j: (0, 0)),
                  pl.BlockSpec((d, tn), lambda j: (0, j))],
        out_specs=pl.BlockSpec((m, tn), lambda j: (0, j)),
        out_shape=jax.ShapeDtypeStruct((m, n), jnp.bfloat16),
        compiler_params=pltpu.CompilerParams(dimension_semantics=("arbitrary",),
                                             vmem_limit_bytes=VMEM_LIMIT),
        name="mem_kv",
    )(mem_h, w_kv_bf16)


def _xattn_tile(q0, q1, z0, z1, kv_ref, o_ref):
    scale = 1.0 / math.sqrt(HEAD_DIM_C)
    q_refs = (q0, q1)
    z_refs = (z0, z1)
    for h in range(N_HEADS_C):
        cb, off = divmod(h * HEAD_DIM_C, COL_BLOCK)
        q = q_refs[cb][:, off:off + HEAD_DIM_C]
        k = kv_ref[:, h * HEAD_DIM_C:(h + 1) * HEAD_DIM_C]
        v = kv_ref[:, WIDTH_C + h * HEAD_DIM_C:WIDTH_C + (h + 1) * HEAD_DIM_C]
        s = _dot_nt(q, k) * scale
        m = jnp.max(s, axis=-1, keepdims=True)
        p = jnp.exp(s - m)
        denom = jnp.sum(p, axis=-1, keepdims=True)
        o = _dot(p.astype(jnp.bfloat16), v) / denom
        z = z_refs[cb][:, off:off + HEAD_DIM_C].astype(jnp.float32)
        o_ref[:, h * HEAD_DIM_C:(h + 1) * HEAD_DIM_C] = (o * z).astype(o_ref.dtype)


def _tail_kernel(x_ref, ga_ref, u_ref, v_ref, zb_ref, qc0_ref, qc1_ref, zc0_ref, zc1_ref, kv_ref,
                 gates_ref, lng_ref, lnb_ref, ws_ref, bs_ref, wa_ref, wb_ref, wc_ref, wo_ref, fg_ref,
                 o_ref, gb_s, gc_s, *, final_norm):
    d = D_MODEL
    merged = _dot(ga_ref[...], wa_ref[...]) * gates_ref[:, 0:d].astype(jnp.float32)
    _gmlp_tile(u_ref, v_ref, zb_ref, lng_ref, lnb_ref, ws_ref, bs_ref, gb_s)
    merged = merged + _dot(gb_s[...], wb_ref[...]) * gates_ref[:, d:2 * d].astype(jnp.float32)
    _xattn_tile(qc0_ref, qc1_ref, zc0_ref, zc1_ref, kv_ref, gc_s)
    merged = merged + _dot(gc_s[...], wc_ref[...]) * gates_ref[:, 2 * d:3 * d].astype(jnp.float32)
    y = x_ref[...] + _dot(merged.astype(jnp.bfloat16), wo_ref[...])
    if final_norm:
        ms = jnp.mean(y * y, axis=-1, keepdims=True)
        y = (y * lax.rsqrt(ms + EPS)) * fg_ref[...]
    o_ref[...] = y


def _tail(x3, ga3, proj3, gates3, kv3, ln_gain, ln_bias, w_spatial_bf16, b_spatial_bcast,
          wa, wb, wc, wo, final_gain, tm, final_norm):
    b, s, d = x3.shape

    def rows(width, idx=0):
        return pl.BlockSpec((None, tm, width), lambda bi, i: (bi, i, idx))

    def resident(shape):
        zeros = (0,) * len(shape)
        return pl.BlockSpec(shape, lambda bi, i: zeros, pipeline_mode=pl.Buffered(1))

    in_specs = [
        rows(d), rows(WIDTH_A),
        rows(WIDTH_B, OUT_U_B_T), rows(WIDTH_B, OUT_V_B_T), rows(WIDTH_B, OUT_Z_B_T),
        rows(COL_BLOCK, OUT_Q_C[0]), rows(COL_BLOCK, OUT_Q_C[1]),
        rows(COL_BLOCK, OUT_Z_C[0]), rows(COL_BLOCK, OUT_Z_C[1]),
        pl.BlockSpec((None, MEM_LEN, 2 * WIDTH_C), lambda bi, i: (bi, 0, 0)),
        rows(N_GATES),
        resident((1, WIDTH_B)), resident((1, WIDTH_B)),
        resident((N_GROUPS_B, CHUNK, CHUNK)), resident((N_GROUPS_B, CHUNK, CHUNK)),
        resident((WIDTH_A, d)), resident((WIDTH_B, d)), resident((WIDTH_C, d)), resident((d, d)),
        resident((1, d)),
    ]
    return pl.pallas_call(
        functools.partial(_tail_kernel, final_norm=final_norm),
        grid=(b, s // tm),
        in_specs=in_specs,
        out_specs=rows(d),
        out_shape=jax.ShapeDtypeStruct((b, s, d), jnp.float32),
        scratch_shapes=[pltpu.VMEM((tm, WIDTH_B), jnp.bfloat16),
                        pltpu.VMEM((tm, WIDTH_C), jnp.bfloat16)],
        compiler_params=pltpu.CompilerParams(dimension_semantics=("arbitrary", "arbitrary"),
                                             vmem_limit_bytes=VMEM_LIMIT),
        name="tail_merge_out",
    )(x3, ga3, proj3, proj3, proj3, proj3, proj3, proj3, proj3, kv3, gates3,
      ln_gain.reshape(1, WIDTH_B), ln_bias.reshape(1, WIDTH_B), w_spatial_bf16, b_spatial_bcast,
      wa, wb, wc, wo, final_gain.reshape(1, d))


def kernel(x, mem, norm_gain, mem_norm_gain, w_in, sink, ln_v_gain, ln_v_bias, w_spatial, b_spatial,
           w_kv_mem, w_br_a, w_br_b, w_br_c, w_out, final_gain):
    b, s, d = x.shape
    depth = w_in.shape[0]
    bf16 = jnp.bfloat16
    x2d = x.reshape(b * s, d)
    mem2d = mem.reshape(b * MEM_LEN, d)
    for l in range(depth):
        h = _rmsnorm_bf16(x2d, norm_gain[l], 512)
        w_in_bf16 = w_in[l].astype(bf16)
        proj = _inproj(h, w_in_bf16, MAIN_SRC_BLOCKS, MAIN_STEP_PATTERN, MAIN_PATTERNS,
                       2048, 512, "inproj_main")
        gates = _inproj(h, w_in_bf16, GATE_SRC_BLOCKS, GATE_STEP_PATTERN, GATE_PATTERNS,
                        2048, 512, "inproj_gates")
        proj3 = proj.reshape(b, s, N_MAIN)

        ga3 = _branch_a(proj3, sink[l].astype(jnp.float32), 512)

        mem_h = _rmsnorm_bf16(mem2d, mem_norm_gain[l], 256)
        kv = _mem_kv(mem_h, w_kv_mem[l].astype(bf16), 512)

        bs_bcast = jnp.broadcast_to(b_spatial[l].astype(jnp.float32)[:, :, None],
                                    (N_GROUPS_B, CHUNK, CHUNK))
        x3 = _tail(x2d.reshape(b, s, d), ga3, proj3, gates.reshape(b, s, N_GATES),
                   kv.reshape(b, MEM_LEN, 2 * WIDTH_C), ln_v_gain[l], ln_v_bias[l],
                   w_spatial[l].astype(bf16), bs_bcast,
                   w_br_a[l].astype(bf16), w_br_b[l].astype(bf16), w_br_c[l].astype(bf16),
                   w_out[l].astype(bf16), final_gain, 256, final_norm=(l == depth - 1))
        x2d = x3.reshape(b * s, d)
    return x2d.reshape(b, s, d)
```

```python
import functools
import math

import jax
import jax.numpy as jnp
import numpy as np
from jax import lax
from jax.experimental import pallas as pl
from jax.experimental.pallas import tpu as pltpu

D_MODEL = 2048
MEM_LEN = 256
EPS = 1e-6
HEAD_DIM_A = 128
WIDTH_A = 1536
N_Q_HEADS_A = 12
N_KV_HEADS_A = 4
REP_A = N_Q_HEADS_A // N_KV_HEADS_A
KV_WIDTH_A = 512
BLOCK = 128
WIDTH_B = 1536
CHUNK = 128
N_GROUPS_B = 12
N_HEADS_C = 4
WIDTH_C = 1024
HEAD_DIM_C = 256
NEG_INF = -1e30
N_GATES = 3 * D_MODEL

SQRT_HALF = math.sqrt(0.5)
LOG2_E = math.log2(math.e)
VMEM_LIMIT = 56 * 1024 * 1024

COL_BLOCK = 512
TRIPLE = 3 * COL_BLOCK
PROJ_SRC_BLOCKS = (0, 1, 2,
                   3, 4, 17,
                   5, 6, 7,
                   14, 15, 16,
                   8, 9, 10,
                   11, 12, 13,
                   18, 19, 20
                   ) + tuple(range(21, 33))
PROJ_PATTERNS = (("plain",) * 3, ("silu",) * 3, ("gelu",) * 3, ("plain", "silu", "silu"),
                 ("sigmoid",) * 3)
PROJ_STEP_PATTERN = (0, 0, 1, 1, 2, 2, 3, 4, 4, 4, 4)
N_PROJ = len(PROJ_SRC_BLOCKS) * COL_BLOCK
OUT_Q_A_T, OUT_Z_A_T, OUT_Z_B_T, OUT_U_B_T, OUT_V_B_T = 0, 2, 3, 4, 5
OUT_GATES_T = (7, 8, 9, 10)
OUT_K_A, OUT_V_A = 3, 4
OUT_Q_C = (5, 18)
OUT_Z_C = (19, 20)


def _alibi_slopes(n):
    def pow2_slopes(m):
        start = 2.0 ** (-8.0 / m)
        return [start ** (i + 1) for i in range(m)]
    if math.log2(n).is_integer():
        s = pow2_slopes(n)
    else:
        c = 2 ** int(math.floor(math.log2(n)))
        s = pow2_slopes(c) + pow2_slopes(2 * c)[0::2][: n - c]
    return [float(v) for v in np.asarray(s, dtype=np.float32)]


ALIBI_SLOPES = _alibi_slopes(N_Q_HEADS_A)


def _sigmoid(x):
    return 0.5 * (1.0 + jnp.tanh(0.5 * x))


def _activation(x, kind):
    if kind == "plain":
        return x
    if kind == "silu":
        return x * _sigmoid(x)
    if kind == "gelu":
        return 0.5 * x * (1.0 + lax.erf(x * SQRT_HALF))
    if kind == "sigmoid":
        return _sigmoid(x)
    raise ValueError(kind)


def _dot(a, b):
    return jnp.dot(a, b, preferred_element_type=jnp.float32)


def _dot_nt(a, b):
    return lax.dot_general(a, b, (((1,), (1,)), ((), ())), preferred_element_type=jnp.float32)


def _rmsnorm_kernel(x_ref, g_ref, o_ref):
    x = x_ref[...]
    ms = jnp.mean(x * x, axis=-1, keepdims=True)
    o_ref[...] = ((x * lax.rsqrt(ms + EPS)) * g_ref[...]).astype(o_ref.dtype)


def _rmsnorm_bf16(x2d, gain, tm):
    t, d = x2d.shape
    return pl.pallas_call(
        _rmsnorm_kernel,
        grid=(t // tm,),
        in_specs=[pl.BlockSpec((tm, d), lambda i: (i, 0)),
                  pl.BlockSpec((1, d), lambda i: (0, 0))],
        out_specs=pl.BlockSpec((tm, d), lambda i: (i, 0)),
        out_shape=jax.ShapeDtypeStruct((t, d), jnp.bfloat16),
        compiler_params=pltpu.CompilerParams(dimension_semantics=("arbitrary",),
                                             vmem_limit_bytes=VMEM_LIMIT),
        name="rmsnorm_bf16",
    )(x2d, gain.reshape(1, d))


def _inproj_kernel(src_ref, pat_ref, h_ref, w0_ref, w1_ref, w2_ref, o_ref, wbuf, *, patterns, chunk):
    jp = pl.program_id(0)
    i = pl.program_id(1)
    tm = h_ref.shape[0]
    slice_rows = w0_ref.shape[0]
    nxt = jp % 2
    cur = (jp + 1) % 2

    def cast_next_weights():
        for s, w_ref in enumerate((w0_ref, w1_ref, w2_ref)):
            wbuf[nxt, pl.ds(pl.multiple_of(i * slice_rows, slice_rows), slice_rows),
                 s * COL_BLOCK:(s + 1) * COL_BLOCK] = w_ref[...].astype(wbuf.dtype)

    @pl.when(jp == 0)
    def _():
        cast_next_weights()

    pat = pat_ref[jnp.maximum(jp - 1, 0)]
    for p, pattern in enumerate(patterns):
        @pl.when((jp > 0) & (pat == p))
        def _(pattern=pattern):
            cast_next_weights()
            for c in range(tm // chunk):
                rows = slice(c * chunk, (c + 1) * chunk)
                hc = h_ref[rows, :]
                for s in range(3):
                    acc = _dot(hc, wbuf[cur, :, s * COL_BLOCK:(s + 1) * COL_BLOCK])
                    o_ref[rows, s * COL_BLOCK:(s + 1) * COL_BLOCK] = _activation(
                        acc, pattern[s]).astype(o_ref.dtype)


def _inproj(h, w_in, src_blocks, step_pattern, patterns, tm, chunk):
    t, d = h.shape
    n_steps = len(step_pattern)
    m_tiles = t // tm
    assert len(src_blocks) == 3 * n_steps and d % m_tiles == 0
    slice_rows = d // m_tiles

    def w_spec(s):
        def imap(jp, i, src_ref, pat_ref):
            return (i, src_ref[3 * jnp.minimum(jp, n_steps - 1) + s])
        return pl.BlockSpec((slice_rows, COL_BLOCK), imap)

    def h_map(jp, i, src_ref, pat_ref):
        return (jnp.where(jp == 0, 0, i), 0)

    def o_map(jp, i, src_ref, pat_ref):
        return (jnp.where(jp == 0, 0, i), jnp.maximum(jp - 1, 0))

    return pl.pallas_call(
        functools.partial(_inproj_kernel, patterns=patterns, chunk=chunk),
        grid_spec=pltpu.PrefetchScalarGridSpec(
            num_scalar_prefetch=2,
            grid=(n_steps + 1, m_tiles),
            in_specs=[pl.BlockSpec((tm, d), h_map), w_spec(0), w_spec(1), w_spec(2)],
            out_specs=pl.BlockSpec((tm, TRIPLE), o_map),
            scratch_shapes=[pltpu.VMEM((2, d, TRIPLE), jnp.bfloat16)],
        ),
        out_shape=jax.ShapeDtypeStruct((t, n_steps * TRIPLE), jnp.bfloat16),
        compiler_params=pltpu.CompilerParams(dimension_semantics=("arbitrary", "arbitrary"),
                                             vmem_limit_bytes=VMEM_LIMIT),
        name="inproj",
    )(jnp.asarray(src_blocks, jnp.int32), jnp.asarray(step_pattern, jnp.int32),
      h, w_in, w_in, w_in)


def _attn_kernel(sink_ref, q_ref, kp_ref, kc_ref, kn_ref, vp_ref, vc_ref, vn_ref, z_ref, o_ref,
                 kbuf, vbuf, bias_ref):
    tq = q_ref.shape[0]
    nblk = tq // BLOCK
    bi = pl.program_id(0)
    n = pl.program_id(1)
    nt = pl.num_programs(1)
    c1 = LOG2_E / math.sqrt(HEAD_DIM_A)

    @pl.when((bi == 0) & (n == 0))
    def _():
        row = lax.broadcasted_iota(jnp.int32, (BLOCK, 3 * BLOCK), 0)
        col = lax.broadcasted_iota(jnp.int32, (BLOCK, 3 * BLOCK), 1)
        dist = jnp.abs(row - (col - BLOCK))
        valid = dist <= BLOCK
        distf = dist.astype(jnp.float32)
        for h in range(N_Q_HEADS_A):
            hk, r = divmod(h, REP_A)
            bias_ref[hk, r * BLOCK:(r + 1) * BLOCK, :] = jnp.where(
                valid, (-ALIBI_SLOPES[h] * LOG2_E) * distf, NEG_INF)
        for hk in range(N_KV_HEADS_A):
            vbuf[:, (2 * hk + 1) * HEAD_DIM_A:(2 * hk + 2) * HEAD_DIM_A] = jnp.ones(
                (tq + 2 * BLOCK, HEAD_DIM_A), vbuf.dtype)

    kbuf[0:BLOCK, :] = kp_ref[...]
    kbuf[BLOCK:BLOCK + tq, :] = kc_ref[...]
    kbuf[BLOCK + tq:, :] = kn_ref[...]
    for hk in range(N_KV_HEADS_A):
        src = slice(hk * HEAD_DIM_A, (hk + 1) * HEAD_DIM_A)
        dst = slice(2 * hk * HEAD_DIM_A, (2 * hk + 1) * HEAD_DIM_A)
        vbuf[0:BLOCK, dst] = vp_ref[:, src]
        vbuf[BLOCK:BLOCK + tq, dst] = vc_ref[:, src]
        vbuf[BLOCK + tq:, dst] = vn_ref[:, src]

    edge_p = jnp.where(n == 0, NEG_INF, 0.0)
    edge_n = jnp.where(n == nt - 1, NEG_INF, 0.0)

    units = [(blk, hk) for blk in range(nblk) for hk in range(N_KV_HEADS_A)]

    def scores(unit):
        blk, hk = unit
        rows = slice(blk * BLOCK, (blk + 1) * BLOCK)
        q = jnp.concatenate(
            [q_ref[rows, (hk * REP_A + r) * HEAD_DIM_A:(hk * REP_A + r + 1) * HEAD_DIM_A]
             for r in range(REP_A)], axis=0)
        k = kbuf[blk * BLOCK:(blk + 3) * BLOCK, hk * HEAD_DIM_A:(hk + 1) * HEAD_DIM_A]
        return _dot_nt(q, k)

    s_next = scores(units[0])
    for i, (blk, hk) in enumerate(units):
        s = s_next
        if i + 1 < len(units):
            s_next = scores(units[i + 1])
        rows = slice(blk * BLOCK, (blk + 1) * BLOCK)
        krows = slice(blk * BLOCK, (blk + 3) * BLOCK)
        t = s * c1 + bias_ref[hk]
        t_p, t_c, t_n = t[:, 0:BLOCK], t[:, BLOCK:2 * BLOCK], t[:, 2 * BLOCK:]
        if blk == 0:
            t_p = t_p + edge_p
        if blk == nblk - 1:
            t_n = t_n + edge_n
        sink2 = jnp.concatenate(
            [jnp.full((BLOCK, BLOCK), sink_ref[hk * REP_A + r] * LOG2_E, jnp.float32)
             for r in range(REP_A)], axis=0)
        m = jnp.max(jnp.maximum(jnp.maximum(t_p, t_c), t_n), axis=-1, keepdims=True)
        m = jnp.maximum(jnp.broadcast_to(m, (REP_A * BLOCK, BLOCK)), sink2)
        p = jnp.concatenate([jnp.exp2(t_p - m), jnp.exp2(t_c - m), jnp.exp2(t_n - m)],
                            axis=1).astype(jnp.bfloat16)
        ov = _dot(p, vbuf[krows, 2 * hk * HEAD_DIM_A:(2 * hk + 2) * HEAD_DIM_A])
        denom = ov[:, HEAD_DIM_A:] + jnp.exp2(sink2 - m)
        o = ov[:, :HEAD_DIM_A] * (1.0 / denom)
        for r in range(REP_A):
            hsl = slice((hk * REP_A + r) * HEAD_DIM_A, (hk * REP_A + r + 1) * HEAD_DIM_A)
            z = z_ref[rows, hsl].astype(jnp.float32)
            o_ref[rows, hsl] = (o[r * BLOCK:(r + 1) * BLOCK] * z).astype(o_ref.dtype)


def _branch_a(proj3, sink, tq):
    b, s, _ = proj3.shape
    nb = s // BLOCK
    per = tq // BLOCK

    def tile(width, idx):
        return pl.BlockSpec((None, tq, width), lambda bi, n, sink_ref: (bi, n, idx))

    def halo(idx, after):
        def imap(bi, n, sink_ref):
            blk = (n + 1) * per if after else n * per - 1
            return (bi, jnp.clip(blk, 0, nb - 1), idx)
        return pl.BlockSpec((None, BLOCK, KV_WIDTH_A), imap)

    in_specs = [
        tile(WIDTH_A, OUT_Q_A_T),
        halo(OUT_K_A, False), tile(KV_WIDTH_A, OUT_K_A), halo(OUT_K_A, True),
        halo(OUT_V_A, False), tile(KV_WIDTH_A, OUT_V_A), halo(OUT_V_A, True),
        tile(WIDTH_A, OUT_Z_A_T),
    ]
    return pl.pallas_call(
        _attn_kernel,
        grid_spec=pltpu.PrefetchScalarGridSpec(
            num_scalar_prefetch=1,
            grid=(b, s // tq),
            in_specs=in_specs,
            out_specs=pl.BlockSpec((None, tq, WIDTH_A), lambda bi, n, sink_ref: (bi, n, 0)),
            scratch_shapes=[pltpu.VMEM((tq + 2 * BLOCK, KV_WIDTH_A), jnp.bfloat16),
                            pltpu.VMEM((tq + 2 * BLOCK, 2 * KV_WIDTH_A), jnp.bfloat16),
                            pltpu.VMEM((N_KV_HEADS_A, REP_A * BLOCK, 3 * BLOCK), jnp.float32)],
        ),
        out_shape=jax.ShapeDtypeStruct((b, s, WIDTH_A), jnp.bfloat16),
        compiler_params=pltpu.CompilerParams(dimension_semantics=("arbitrary", "arbitrary"),
                                             vmem_limit_bytes=VMEM_LIMIT),
        name="branch_a_attn",
    )(sink, *([proj3] * 8))


def _gmlp_tile(u_ref, v_ref, z_ref, lng_ref, lnb_ref, ws_ref, bs_ref, o_ref):
    tm = o_ref.shape[0]
    nc = tm // CHUNK
    gv = v_ref[...].astype(jnp.float32)
    mu = jnp.mean(gv, axis=-1, keepdims=True)
    xc = gv - mu
    var = jnp.mean(xc * xc, axis=-1, keepdims=True)
    vn = (xc * lax.rsqrt(var + EPS) * lng_ref[...] + lnb_ref[...]).astype(jnp.bfloat16)
    for g in range(N_GROUPS_B):
        gsl = slice(g * CHUNK, (g + 1) * CHUNK)
        rhs = jnp.concatenate([vn[c * CHUNK:(c + 1) * CHUNK, gsl] for c in range(nc)], axis=1)
        sg = _dot(ws_ref[g], rhs)
        bias = bs_ref[g]
        for c in range(nc):
            rows = slice(c * CHUNK, (c + 1) * CHUNK)
            s_c = sg[:, c * CHUNK:(c + 1) * CHUNK] + bias
            u = u_ref[rows, gsl].astype(jnp.float32)
            z = z_ref[rows, gsl].astype(jnp.float32)
            o_ref[rows, gsl] = (u * s_c * z).astype(o_ref.dtype)


def _matmul_kernel(a_ref, b_ref, o_ref):
    o_ref[...] = _dot(a_ref[...], b_ref[...]).astype(o_ref.dtype)


def _mem_kv(mem_h, w_kv_bf16, tn):
    m, d = mem_h.shape
    n = w_kv_bf16.shape[1]
    return pl.pallas_call(
        _matmul_kernel,
        grid=(n // tn,),
        in_specs=[pl.BlockSpec((m, d), lambda j: (0, 0)),
                  pl.BlockSpec((d, tn), lambda j: (0, j))],
        out_specs=pl.BlockSpec((m, tn), lambda j: (0, j)),
        out_shape=jax.ShapeDtypeStruct((m, n), jnp.bfloat16),
        compiler_params=pltpu.CompilerParams(dimension_semantics=("arbitrary",),
                                             vmem_limit_bytes=VMEM_LIMIT),
        name="mem_kv",
    )(mem_h, w_kv_bf16)


def _xattn_tile(q0, q1, z0, z1, kv_ref, o_ref):
    scale = 1.0 / math.sqrt(HEAD_DIM_C)
    q_refs = (q0, q1)
    z_refs = (z0, z1)
    for h in range(N_HEADS_C):
        cb, off = divmod(h * HEAD_DIM_C, COL_BLOCK)
        q = q_refs[cb][:, off:off + HEAD_DIM_C]
        k = kv_ref[:, h * HEAD_DIM_C:(h + 1) * HEAD_DIM_C]
        v = kv_ref[:, WIDTH_C + h * HEAD_DIM_C:WIDTH_C + (h + 1) * HEAD_DIM_C]
        s = _dot_nt(q, k) * scale
        m = jnp.max(s, axis=-1, keepdims=True)
        p = jnp.exp(s - m)
        denom = jnp.sum(p, axis=-1, keepdims=True)
        o = _dot(p.astype(jnp.bfloat16), v) / denom
        z = z_refs[cb][:, off:off + HEAD_DIM_C].astype(jnp.float32)
        o_ref[:, h * HEAD_DIM_C:(h + 1) * HEAD_DIM_C] = (o * z).astype(o_ref.dtype)


def _tail_kernel(x_ref, ga_ref, u_ref, v_ref, zb_ref, qc0_ref, qc1_ref, zc0_ref, zc1_ref, kv_ref,
                 g0_ref, g1_ref, g2_ref, g3_ref, lng_ref, lnb_ref, ws_ref, bs_ref,
                 wa_ref, wb_ref, wc_ref, wo_ref, fg_ref, o_ref, gb_s, gc_s, *, final_norm):
    g_refs = (g0_ref, g1_ref, g2_ref, g3_ref)

    def gate(branch):
        parts = []
        for cb in range(branch * D_MODEL // COL_BLOCK, (branch + 1) * D_MODEL // COL_BLOCK):
            blk, off = divmod(cb * COL_BLOCK, TRIPLE)
            parts.append(g_refs[blk][:, off:off + COL_BLOCK])
        return jnp.concatenate(parts, axis=1).astype(jnp.float32)

    merged = _dot(ga_ref[...], wa_ref[...]) * gate(0)
    _gmlp_tile(u_ref, v_ref, zb_ref, lng_ref, lnb_ref, ws_ref, bs_ref, gb_s)
    merged = merged + _dot(gb_s[...], wb_ref[...]) * gate(1)
    _xattn_tile(qc0_ref, qc1_ref, zc0_ref, zc1_ref, kv_ref, gc_s)
    merged = merged + _dot(gc_s[...], wc_ref[...]) * gate(2)
    y = x_ref[...] + _dot(merged.astype(jnp.bfloat16), wo_ref[...])
    if final_norm:
        ms = jnp.mean(y * y, axis=-1, keepdims=True)
        y = (y * lax.rsqrt(ms + EPS)) * fg_ref[...]
    o_ref[...] = y


def _tail(x3, ga3, proj3, kv3, ln_gain, ln_bias, w_spatial_bf16, b_spatial_bcast,
          wa, wb, wc, wo, final_gain, tm, final_norm):
    b, s, d = x3.shape

    def rows(width, idx=0):
        return pl.BlockSpec((None, tm, width), lambda bi, i: (bi, i, idx))

    def resident(shape):
        zeros = (0,) * len(shape)
        return pl.BlockSpec(shape, lambda bi, i: zeros, pipeline_mode=pl.Buffered(1))

    in_specs = [
        rows(d), rows(WIDTH_A),
        rows(WIDTH_B, OUT_U_B_T), rows(WIDTH_B, OUT_V_B_T), rows(WIDTH_B, OUT_Z_B_T),
        rows(COL_BLOCK, OUT_Q_C[0]), rows(COL_BLOCK, OUT_Q_C[1]),
        rows(COL_BLOCK, OUT_Z_C[0]), rows(COL_BLOCK, OUT_Z_C[1]),
        pl.BlockSpec((None, MEM_LEN, 2 * WIDTH_C), lambda bi, i: (bi, 0, 0)),
        rows(TRIPLE, OUT_GATES_T[0]), rows(TRIPLE, OUT_GATES_T[1]),
        rows(TRIPLE, OUT_GATES_T[2]), rows(TRIPLE, OUT_GATES_T[3]),
        resident((1, WIDTH_B)), resident((1, WIDTH_B)),
        resident((N_GROUPS_B, CHUNK, CHUNK)), resident((N_GROUPS_B, CHUNK, CHUNK)),
        resident((WIDTH_A, d)), resident((WIDTH_B, d)), resident((WIDTH_C, d)), resident((d, d)),
        resident((1, d)),
    ]
    return pl.pallas_call(
        functools.partial(_tail_kernel, final_norm=final_norm),
        grid=(b, s // tm),
        in_specs=in_specs,
        out_specs=rows(d),
        out_shape=jax.ShapeDtypeStruct((b, s, d), jnp.float32),
        scratch_shapes=[pltpu.VMEM((tm, WIDTH_B), jnp.bfloat16),
                        pltpu.VMEM((tm, WIDTH_C), jnp.bfloat16)],
        compiler_params=pltpu.CompilerParams(dimension_semantics=("arbitrary", "arbitrary"),
                                             vmem_limit_bytes=VMEM_LIMIT),
        name="tail_merge_out",
    )(x3, ga3, *([proj3] * 7), kv3, *([proj3] * 4),
      ln_gain.reshape(1, WIDTH_B), ln_bias.reshape(1, WIDTH_B), w_spatial_bf16, b_spatial_bcast,
      wa, wb, wc, wo, final_gain.reshape(1, d))


def kernel(x, mem, norm_gain, mem_norm_gain, w_in, sink, ln_v_gain, ln_v_bias, w_spatial, b_spatial,
           w_kv_mem, w_br_a, w_br_b, w_br_c, w_out, final_gain):
    b, s, d = x.shape
    depth = w_in.shape[0]
    bf16 = jnp.bfloat16
    x2d = x.reshape(b * s, d)
    mem2d = mem.reshape(b * MEM_LEN, d)
    for l in range(depth):
        h = _rmsnorm_bf16(x2d, norm_gain[l], 512)
        proj = _inproj(h, w_in[l], PROJ_SRC_BLOCKS, PROJ_STEP_PATTERN, PROJ_PATTERNS, 2048, 512)
        proj3 = proj.reshape(b, s, N_PROJ)

        ga3 = _branch_a(proj3, sink[l].astype(jnp.float32), 512)

        mem_h = _rmsnorm_bf16(mem2d, mem_norm_gain[l], 256)
        kv = _mem_kv(mem_h, w_kv_mem[l].astype(bf16), 512)

        bs_bcast = jnp.broadcast_to(b_spatial[l].astype(jnp.float32)[:, :, None],
                                    (N_GROUPS_B, CHUNK, CHUNK))
        x3 = _tail(x2d.reshape(b, s, d), ga3, proj3,
                   kv.reshape(b, MEM_LEN, 2 * WIDTH_C), ln_v_gain[l], ln_v_bias[l],
                   w_spatial[l].astype(bf16), bs_bcast,
                   w_br_a[l].astype(bf16), w_br_b[l].astype(bf16), w_br_c[l].astype(bf16),
                   w_out[l].astype(bf16), final_gain, 256, final_norm=(l == depth - 1))
        x2d = x3.reshape(b * s, d)
    return x2d.reshape(b, s, d)
```

```python
import functools
import math

import jax
import jax.numpy as jnp
import numpy as np
from jax import lax
from jax.experimental import pallas as pl
from jax.experimental.pallas import tpu as pltpu

D_MODEL = 2048
MEM_LEN = 256
EPS = 1e-6
HEAD_DIM_A = 128
WIDTH_A = 1536
N_Q_HEADS_A = 12
N_KV_HEADS_A = 4
REP_A = N_Q_HEADS_A // N_KV_HEADS_A
KV_WIDTH_A = 512
BLOCK = 128
WIDTH_B = 1536
CHUNK = 128
N_GROUPS_B = 12
N_HEADS_C = 4
WIDTH_C = 1024
HEAD_DIM_C = 256
NEG_INF = -1e30
N_GATES = 3 * D_MODEL

SQRT_HALF = math.sqrt(0.5)
LOG2_E = math.log2(math.e)
VMEM_LIMIT = 56 * 1024 * 1024

COL_BLOCK = 512
TRIPLE = 3 * COL_BLOCK
PROJ_SRC_BLOCKS = (0, 1, 2,
                   3, 4, 17,
                   5, 6, 7,
                   14, 15, 16,
                   8, 9, 10,
                   11, 12, 13,
                   18, 19, 20
                   ) + tuple(range(21, 33))
PROJ_PATTERNS = (("plain",) * 3, ("silu",) * 3, ("gelu",) * 3, ("plain", "silu", "silu"),
                 ("sigmoid",) * 3)
PROJ_STEP_PATTERN = (0, 0, 1, 1, 2, 2, 3, 4, 4, 4, 4)
N_PROJ = len(PROJ_SRC_BLOCKS) * COL_BLOCK
INPROJ_CHUNKS_PER_ITER = 2
OUT_Q_A_T, OUT_Z_A_T, OUT_Z_B_T, OUT_U_B_T, OUT_V_B_T = 0, 2, 3, 4, 5
OUT_GATES_T = (7, 8, 9, 10)
OUT_K_A, OUT_V_A = 3, 4
OUT_Q_C = (5, 18)
OUT_Z_C = (19, 20)


def _alibi_slopes(n):
    def pow2_slopes(m):
        start = 2.0 ** (-8.0 / m)
        return [start ** (i + 1) for i in range(m)]
    if math.log2(n).is_integer():
        s = pow2_slopes(n)
    else:
        c = 2 ** int(math.floor(math.log2(n)))
        s = pow2_slopes(c) + pow2_slopes(2 * c)[0::2][: n - c]
    return [float(v) for v in np.asarray(s, dtype=np.float32)]


ALIBI_SLOPES = _alibi_slopes(N_Q_HEADS_A)


def _sigmoid(x):
    return 0.5 * (1.0 + jnp.tanh(0.5 * x))


def _activation(x, kind):
    if kind == "plain":
        return x
    if kind == "silu":
        return x * _sigmoid(x)
    if kind == "gelu":
        return 0.5 * x * (1.0 + lax.erf(x * SQRT_HALF))
    if kind == "sigmoid":
        return _sigmoid(x)
    raise ValueError(kind)


def _dot(a, b):
    return jnp.dot(a, b, preferred_element_type=jnp.float32)


def _dot_nt(a, b):
    return lax.dot_general(a, b, (((1,), (1,)), ((), ())), preferred_element_type=jnp.float32)


def _rmsnorm_kernel(x_ref, g_ref, o_ref):
    x = x_ref[...]
    ms = jnp.mean(x * x, axis=-1, keepdims=True)
    o_ref[...] = ((x * lax.rsqrt(ms + EPS)) * g_ref[...]).astype(o_ref.dtype)


def _rmsnorm_bf16(x2d, gain, tm):
    t, d = x2d.shape
    return pl.pallas_call(
        _rmsnorm_kernel,
        grid=(t // tm,),
        in_specs=[pl.BlockSpec((tm, d), lambda i: (i, 0)),
                  pl.BlockSpec((1, d), lambda i: (0, 0))],
        out_specs=pl.BlockSpec((tm, d), lambda i: (i, 0)),
        out_shape=jax.ShapeDtypeStruct((t, d), jnp.bfloat16),
        compiler_params=pltpu.CompilerParams(dimension_semantics=("arbitrary",),
                                             vmem_limit_bytes=VMEM_LIMIT),
        name="rmsnorm_bf16",
    )(x2d, gain.reshape(1, d))


def _inproj_kernel(src_ref, pat_ref, h_ref, w0_ref, w1_ref, w2_ref, o_ref, wbuf, *, patterns, chunk):
    jp = pl.program_id(0)
    i = pl.program_id(1)
    tm = h_ref.shape[0]
    slice_rows = w0_ref.shape[0]
    nxt = jp % 2
    cur = (jp + 1) % 2

    n_iter = tm // (INPROJ_CHUNKS_PER_ITER * chunk)
    part_rows = slice_rows // n_iter

    def cast_next_weights(part):
        src_rows = pl.ds(pl.multiple_of(part * part_rows, part_rows), part_rows)
        dst_rows = pl.ds(pl.multiple_of(i * slice_rows + part * part_rows, part_rows), part_rows)
        for s, w_ref in enumerate((w0_ref, w1_ref, w2_ref)):
            wbuf[nxt, dst_rows, s * COL_BLOCK:(s + 1) * COL_BLOCK] = (
                w_ref[src_rows, :].astype(wbuf.dtype))

    @pl.when(jp == 0)
    def _():
        for part in range(n_iter):
            cast_next_weights(part)

    pat = pat_ref[jnp.maximum(jp - 1, 0)]
    for p, pattern in enumerate(patterns):
        @pl.when((jp > 0) & (pat == p))
        def _(pattern=pattern):
            def body(it, carry):
                cast_next_weights(it)
                for c in range(INPROJ_CHUNKS_PER_ITER):
                    start = pl.multiple_of((it * INPROJ_CHUNKS_PER_ITER + c) * chunk, chunk)
                    rows = pl.ds(start, chunk)
                    hc = h_ref[rows, :]
                    for s in range(3):
                        acc = _dot(hc, wbuf[cur, :, s * COL_BLOCK:(s + 1) * COL_BLOCK])
                        o_ref[rows, s * COL_BLOCK:(s + 1) * COL_BLOCK] = _activation(
                            acc, pattern[s]).astype(o_ref.dtype)
                return carry

            lax.fori_loop(0, n_iter, body, 0)


def _inproj(h, w_in, src_blocks, step_pattern, patterns, tm, chunk):
    t, d = h.shape
    n_steps = len(step_pattern)
    m_tiles = t // tm
    assert len(src_blocks) == 3 * n_steps and d % m_tiles == 0
    slice_rows = d // m_tiles

    def w_spec(s):
        def imap(jp, i, src_ref, pat_ref):
            return (i, src_ref[3 * jnp.minimum(jp, n_steps - 1) + s])
        return pl.BlockSpec((slice_rows, COL_BLOCK), imap)

    def h_map(jp, i, src_ref, pat_ref):
        return (jnp.where(jp == 0, 0, i), 0)

    def o_map(jp, i, src_ref, pat_ref):
        return (jnp.where(jp == 0, 0, i), jnp.maximum(jp - 1, 0))

    return pl.pallas_call(
        functools.partial(_inproj_kernel, patterns=patterns, chunk=chunk),
        grid_spec=pltpu.PrefetchScalarGridSpec(
            num_scalar_prefetch=2,
            grid=(n_steps + 1, m_tiles),
            in_specs=[pl.BlockSpec((tm, d), h_map), w_spec(0), w_spec(1), w_spec(2)],
            out_specs=pl.BlockSpec((tm, TRIPLE), o_map),
            scratch_shapes=[pltpu.VMEM((2, d, TRIPLE), jnp.bfloat16)],
        ),
        out_shape=jax.ShapeDtypeStruct((t, n_steps * TRIPLE), jnp.bfloat16),
        compiler_params=pltpu.CompilerParams(dimension_semantics=("arbitrary", "arbitrary"),
                                             vmem_limit_bytes=VMEM_LIMIT),
        name="inproj",
    )(jnp.asarray(src_blocks, jnp.int32), jnp.asarray(step_pattern, jnp.int32),
      h, w_in, w_in, w_in)


def _attn_kernel(sink_ref, q_ref, kp_ref, kc_ref, kn_ref, vp_ref, vc_ref, vn_ref, z_ref, o_ref,
                 kbuf, vbuf, bias_ref):
    tq = q_ref.shape[0]
    nblk = tq // BLOCK
    bi = pl.program_id(0)
    n = pl.program_id(1)
    nt = pl.num_programs(1)
    c1 = LOG2_E / math.sqrt(HEAD_DIM_A)

    @pl.when((bi == 0) & (n == 0))
    def _():
        row = lax.broadcasted_iota(jnp.int32, (BLOCK, 3 * BLOCK), 0)
        col = lax.broadcasted_iota(jnp.int32, (BLOCK, 3 * BLOCK), 1)
        dist = jnp.abs(row - (col - BLOCK))
        valid = dist <= BLOCK
        distf = dist.astype(jnp.float32)
        for h in range(N_Q_HEADS_A):
            hk, r = divmod(h, REP_A)
            bias_ref[hk, r * BLOCK:(r + 1) * BLOCK, :] = jnp.where(
                valid, (-ALIBI_SLOPES[h] * LOG2_E) * distf, NEG_INF)
        for hk in range(N_KV_HEADS_A):
            vbuf[:, (2 * hk + 1) * HEAD_DIM_A:(2 * hk + 2) * HEAD_DIM_A] = jnp.ones(
                (tq + 2 * BLOCK, HEAD_DIM_A), vbuf.dtype)

    kbuf[0:BLOCK, :] = kp_ref[...]
    kbuf[BLOCK:BLOCK + tq, :] = kc_ref[...]
    kbuf[BLOCK + tq:, :] = kn_ref[...]
    for hk in range(N_KV_HEADS_A):
        src = slice(hk * HEAD_DIM_A, (hk + 1) * HEAD_DIM_A)
        dst = slice(2 * hk * HEAD_DIM_A, (2 * hk + 1) * HEAD_DIM_A)
        vbuf[0:BLOCK, dst] = vp_ref[:, src]
        vbuf[BLOCK:BLOCK + tq, dst] = vc_ref[:, src]
        vbuf[BLOCK + tq:, dst] = vn_ref[:, src]

    edge_p = jnp.where(n == 0, NEG_INF, 0.0)
    edge_n = jnp.where(n == nt - 1, NEG_INF, 0.0)

    units = [(blk, hk) for blk in range(nblk) for hk in range(N_KV_HEADS_A)]

    def scores(unit):
        blk, hk = unit
        rows = slice(blk * BLOCK, (blk + 1) * BLOCK)
        q = jnp.concatenate(
            [q_ref[rows, (hk * REP_A + r) * HEAD_DIM_A:(hk * REP_A + r + 1) * HEAD_DIM_A]
             for r in range(REP_A)], axis=0)
        k = kbuf[blk * BLOCK:(blk + 3) * BLOCK, hk * HEAD_DIM_A:(hk + 1) * HEAD_DIM_A]
        return _dot_nt(q, k)

    s_next = scores(units[0])
    for i, (blk, hk) in enumerate(units):
        s = s_next
        if i + 1 < len(units):
            s_next = scores(units[i + 1])
        rows = slice(blk * BLOCK, (blk + 1) * BLOCK)
        krows = slice(blk * BLOCK, (blk + 3) * BLOCK)
        t = s * c1 + bias_ref[hk]
        t_p, t_c, t_n = t[:, 0:BLOCK], t[:, BLOCK:2 * BLOCK], t[:, 2 * BLOCK:]
        if blk == 0:
            t_p = t_p + edge_p
        if blk == nblk - 1:
            t_n = t_n + edge_n
        sink2 = jnp.concatenate(
            [jnp.full((BLOCK, BLOCK), sink_ref[hk * REP_A + r] * LOG2_E, jnp.float32)
             for r in range(REP_A)], axis=0)
        m = jnp.max(jnp.maximum(jnp.maximum(t_p, t_c), t_n), axis=-1, keepdims=True)
        m = jnp.maximum(jnp.broadcast_to(m, (REP_A * BLOCK, BLOCK)), sink2)
        p = jnp.concatenate([jnp.exp2(t_p - m), jnp.exp2(t_c - m), jnp.exp2(t_n - m)],
                            axis=1).astype(jnp.bfloat16)
        ov = _dot(p, vbuf[krows, 2 * hk * HEAD_DIM_A:(2 * hk + 2) * HEAD_DIM_A])
        denom = ov[:, HEAD_DIM_A:] + jnp.exp2(sink2 - m)
        o = ov[:, :HEAD_DIM_A] * (1.0 / denom)
        for r in range(REP_A):
            hsl = slice((hk * REP_A + r) * HEAD_DIM_A, (hk * REP_A + r + 1) * HEAD_DIM_A)
            z = z_ref[rows, hsl].astype(jnp.float32)
            o_ref[rows, hsl] = (o[r * BLOCK:(r + 1) * BLOCK] * z).astype(o_ref.dtype)


def _branch_a(proj3, sink, tq):
    b, s, _ = proj3.shape
    nb = s // BLOCK
    per = tq // BLOCK

    def tile(width, idx):
        return pl.BlockSpec((None, tq, width), lambda bi, n, sink_ref: (bi, n, idx))

    def halo(idx, after):
        def imap(bi, n, sink_ref):
            blk = (n + 1) * per if after else n * per - 1
            return (bi, jnp.clip(blk, 0, nb - 1), idx)
        return pl.BlockSpec((None, BLOCK, KV_WIDTH_A), imap)

    in_specs = [
        tile(WIDTH_A, OUT_Q_A_T),
        halo(OUT_K_A, False), tile(KV_WIDTH_A, OUT_K_A), halo(OUT_K_A, True),
        halo(OUT_V_A, False), tile(KV_WIDTH_A, OUT_V_A), halo(OUT_V_A, True),
        tile(WIDTH_A, OUT_Z_A_T),
    ]
    return pl.pallas_call(
        _attn_kernel,
        grid_spec=pltpu.PrefetchScalarGridSpec(
            num_scalar_prefetch=1,
            grid=(b, s // tq),
            in_specs=in_specs,
            out_specs=pl.BlockSpec((None, tq, WIDTH_A), lambda bi, n, sink_ref: (bi, n, 0)),
            scratch_shapes=[pltpu.VMEM((tq + 2 * BLOCK, KV_WIDTH_A), jnp.bfloat16),
                            pltpu.VMEM((tq + 2 * BLOCK, 2 * KV_WIDTH_A), jnp.bfloat16),
                            pltpu.VMEM((N_KV_HEADS_A, REP_A * BLOCK, 3 * BLOCK), jnp.float32)],
        ),
        out_shape=jax.ShapeDtypeStruct((b, s, WIDTH_A), jnp.bfloat16),
        compiler_params=pltpu.CompilerParams(dimension_semantics=("arbitrary", "arbitrary"),
                                             vmem_limit_bytes=VMEM_LIMIT),
        name="branch_a_attn",
    )(sink, *([proj3] * 8))


def _gmlp_tile(u_ref, v_ref, z_ref, lng_ref, lnb_ref, ws_ref, bs_ref, o_ref):
    tm = o_ref.shape[0]
    nc = tm // CHUNK
    gv = v_ref[...].astype(jnp.float32)
    mu = jnp.mean(gv, axis=-1, keepdims=True)
    xc = gv - mu
    var = jnp.mean(xc * xc, axis=-1, keepdims=True)
    vn = (xc * lax.rsqrt(var + EPS) * lng_ref[...] + lnb_ref[...]).astype(jnp.bfloat16)
    for g in range(N_GROUPS_B):
        gsl = slice(g * CHUNK, (g + 1) * CHUNK)
        rhs = jnp.concatenate([vn[c * CHUNK:(c + 1) * CHUNK, gsl] for c in range(nc)], axis=1)
        sg = _dot(ws_ref[g], rhs)
        bias = bs_ref[g]
        for c in range(nc):
            rows = slice(c * CHUNK, (c + 1) * CHUNK)
            s_c = sg[:, c * CHUNK:(c + 1) * CHUNK] + bias
            u = u_ref[rows, gsl].astype(jnp.float32)
            z = z_ref[rows, gsl].astype(jnp.float32)
            o_ref[rows, gsl] = (u * s_c * z).astype(o_ref.dtype)


def _matmul_kernel(a_ref, b_ref, o_ref):
    o_ref[...] = _dot(a_ref[...], b_ref[...]).astype(o_ref.dtype)


def _mem_kv(mem_h, w_kv_bf16, tn):
    m, d = mem_h.shape
    n = w_kv_bf16.shape[1]
    return pl.pallas_call(
        _matmul_kernel,
        grid=(n // tn,),
        in_specs=[pl.BlockSpec((m, d), lambda j: (0, 0)),
                  pl.BlockSpec((d, tn), lambda j: (0, j))],
        out_specs=pl.BlockSpec((m, tn), lambda j: (0, j)),
        out_shape=jax.ShapeDtypeStruct((m, n), jnp.bfloat16),
        compiler_params=pltpu.CompilerParams(dimension_semantics=("arbitrary",),
                                             vmem_limit_bytes=VMEM_LIMIT),
        name="mem_kv",
    )(mem_h, w_kv_bf16)


def _xattn_tile(q0, q1, z0, z1, kv_ref, o_ref):
    scale = 1.0 / math.sqrt(HEAD_DIM_C)
    q_refs = (q0, q1)
    z_refs = (z0, z1)
    for h in range(N_HEADS_C):
        cb, off = divmod(h * HEAD_DIM_C, COL_BLOCK)
        q = q_refs[cb][:, off:off + HEAD_DIM_C]
        k = kv_ref[:, h * HEAD_DIM_C:(h + 1) * HEAD_DIM_C]
        v = kv_ref[:, WIDTH_C + h * HEAD_DIM_C:WIDTH_C + (h + 1) * HEAD_DIM_C]
        s = _dot_nt(q, k) * scale
        m = jnp.max(s, axis=-1, keepdims=True)
        p = jnp.exp(s - m)
        denom = jnp.sum(p, axis=-1, keepdims=True)
        o = _dot(p.astype(jnp.bfloat16), v) / denom
        z = z_refs[cb][:, off:off + HEAD_DIM_C].astype(jnp.float32)
        o_ref[:, h * HEAD_DIM_C:(h + 1) * HEAD_DIM_C] = (o * z).astype(o_ref.dtype)


def _tail_kernel(x_ref, ga_ref, u_ref, v_ref, zb_ref, qc0_ref, qc1_ref, zc0_ref, zc1_ref, kv_ref,
                 g0_ref, g1_ref, g2_ref, g3_ref, lng_ref, lnb_ref, ws_ref, bs_ref,
                 wa_ref, wb_ref, wc_ref, wo_ref, fg_ref, o_ref, gb_s, gc_s, *, final_norm):
    g_refs = (g0_ref, g1_ref, g2_ref, g3_ref)

    def gate(branch):
        parts = []
        for cb in range(branch * D_MODEL // COL_BLOCK, (branch + 1) * D_MODEL // COL_BLOCK):
            blk, off = divmod(cb * COL_BLOCK, TRIPLE)
            parts.append(g_refs[blk][:, off:off + COL_BLOCK])
        return jnp.concatenate(parts, axis=1).astype(jnp.float32)

    merged = _dot(ga_ref[...], wa_ref[...]) * gate(0)
    _gmlp_tile(u_ref, v_ref, zb_ref, lng_ref, lnb_ref, ws_ref, bs_ref, gb_s)
    merged = merged + _dot(gb_s[...], wb_ref[...]) * gate(1)
    _xattn_tile(qc0_ref, qc1_ref, zc0_ref, zc1_ref, kv_ref, gc_s)
    merged = merged + _dot(gc_s[...], wc_ref[...]) * gate(2)
    y = x_ref[...] + _dot(merged.astype(jnp.bfloat16), wo_ref[...])
    if final_norm:
        ms = jnp.mean(y * y, axis=-1, keepdims=True)
        y = (y * lax.rsqrt(ms + EPS)) * fg_ref[...]
    o_ref[...] = y


def _tail(x3, ga3, proj3, kv3, ln_gain, ln_bias, w_spatial_bf16, b_spatial_bcast,
          wa, wb, wc, wo, final_gain, tm, final_norm):
    b, s, d = x3.shape

    def rows(width, idx=0):
        return pl.BlockSpec((None, tm, width), lambda bi, i: (bi, i, idx))

    def resident(shape):
        zeros = (0,) * len(shape)
        return pl.BlockSpec(shape, lambda bi, i: zeros, pipeline_mode=pl.Buffered(1))

    in_specs = [
        rows(d), rows(WIDTH_A),
        rows(WIDTH_B, OUT_U_B_T), rows(WIDTH_B, OUT_V_B_T), rows(WIDTH_B, OUT_Z_B_T),
        rows(COL_BLOCK, OUT_Q_C[0]), rows(COL_BLOCK, OUT_Q_C[1]),
        rows(COL_BLOCK, OUT_Z_C[0]), rows(COL_BLOCK, OUT_Z_C[1]),
        pl.BlockSpec((None, MEM_LEN, 2 * WIDTH_C), lambda bi, i: (bi, 0, 0)),
        rows(TRIPLE, OUT_GATES_T[0]), rows(TRIPLE, OUT_GATES_T[1]),
        rows(TRIPLE, OUT_GATES_T[2]), rows(TRIPLE, OUT_GATES_T[3]),
        resident((1, WIDTH_B)), resident((1, WIDTH_B)),
        resident((N_GROUPS_B, CHUNK, CHUNK)), resident((N_GROUPS_B, CHUNK, CHUNK)),
        resident((WIDTH_A, d)), resident((WIDTH_B, d)), resident((WIDTH_C, d)), resident((d, d)),
        resident((1, d)),
    ]
    return pl.pallas_call(
        functools.partial(_tail_kernel, final_norm=final_norm),
        grid=(b, s // tm),
        in_specs=in_specs,
        out_specs=rows(d),
        out_shape=jax.ShapeDtypeStruct((b, s, d), jnp.float32),
        scratch_shapes=[pltpu.VMEM((tm, WIDTH_B), jnp.bfloat16),
                        pltpu.VMEM((tm, WIDTH_C), jnp.bfloat16)],
        compiler_params=pltpu.CompilerParams(dimension_semantics=("arbitrary", "arbitrary"),
                                             vmem_limit_bytes=VMEM_LIMIT),
        name="tail_merge_out",
    )(x3, ga3, *([proj3] * 7), kv3, *([proj3] * 4),
      ln_gain.reshape(1, WIDTH_B), ln_bias.reshape(1, WIDTH_B), w_spatial_bf16, b_spatial_bcast,
      wa, wb, wc, wo, final_gain.reshape(1, d))


def kernel(x, mem, norm_gain, mem_norm_gain, w_in, sink, ln_v_gain, ln_v_bias, w_spatial, b_spatial,
           w_kv_mem, w_br_a, w_br_b, w_br_c, w_out, final_gain):
    b, s, d = x.shape
    depth = w_in.shape[0]
    bf16 = jnp.bfloat16
    x2d = x.reshape(b * s, d)
    mem2d = mem.reshape(b * MEM_LEN, d)
    for l in range(depth):
        h = _rmsnorm_bf16(x2d, norm_gain[l], 512)
        proj = _inproj(h, w_in[l], PROJ_SRC_BLOCKS, PROJ_STEP_PATTERN, PROJ_PATTERNS, 2048, 512)
        proj3 = proj.reshape(b, s, N_PROJ)

        ga3 = _branch_a(proj3, sink[l].astype(jnp.float32), 512)

        mem_h = _rmsnorm_bf16(mem2d, mem_norm_gain[l], 256)
        kv = _mem_kv(mem_h, w_kv_mem[l].astype(bf16), 512)

        bs_bcast = jnp.broadcast_to(b_spatial[l].astype(jnp.float32)[:, :, None],
                                    (N_GROUPS_B, CHUNK, CHUNK))
        x3 = _tail(x2d.reshape(b, s, d), ga3, proj3,
                   kv.reshape(b, MEM_LEN, 2 * WIDTH_C), ln_v_gain[l], ln_v_bias[l],
                   w_spatial[l].astype(bf16), bs_bcast,
                   w_br_a[l].astype(bf16), w_br_b[l].astype(bf16), w_br_c[l].astype(bf16),
                   w_out[l].astype(bf16), final_gain, 256, final_norm=(l == depth - 1))
        x2d = x3.reshape(b * s, d)
    return x2d.reshape(b, s, d)
```

```python
import functools
import math

import jax
import jax.numpy as jnp
import numpy as np
from jax import lax
from jax.experimental import pallas as pl
from jax.experimental.pallas import tpu as pltpu

D_MODEL = 2048
MEM_LEN = 256
EPS = 1e-6
HEAD_DIM_A = 128
WIDTH_A = 1536
N_Q_HEADS_A = 12
N_KV_HEADS_A = 4
REP_A = N_Q_HEADS_A // N_KV_HEADS_A
KV_WIDTH_A = 512
BLOCK = 128
WIDTH_B = 1536
CHUNK = 128
N_GROUPS_B = 12
N_HEADS_C = 4
WIDTH_C = 1024
HEAD_DIM_C = 256
NEG_INF = -1e30
N_GATES = 3 * D_MODEL

SQRT_HALF = math.sqrt(0.5)
LOG2_E = math.log2(math.e)
VMEM_LIMIT = 56 * 1024 * 1024

COL_BLOCK = 512
TRIPLE = 3 * COL_BLOCK
PROJ_SRC_BLOCKS = (0, 1, 2,
                   3, 4, 17,
                   5, 6, 7,
                   14, 15, 16,
                   8, 9, 10,
                   11, 12, 13,
                   18, 19, 20
                   ) + tuple(range(21, 33))
PROJ_PATTERNS = (("plain",) * 3, ("silu",) * 3, ("gelu",) * 3, ("plain", "silu", "silu"),
                 ("sigmoid",) * 3)
PROJ_STEP_PATTERN = (0, 0, 1, 1, 2, 2, 3, 4, 4, 4, 4)
N_PROJ = len(PROJ_SRC_BLOCKS) * COL_BLOCK
INPROJ_CHUNKS_PER_ITER = 2
OUT_Q_A_T, OUT_Z_A_T, OUT_Z_B_T, OUT_U_B_T, OUT_V_B_T = 0, 2, 3, 4, 5
OUT_GATES_T = (7, 8, 9, 10)
OUT_K_A, OUT_V_A = 3, 4
OUT_Q_C = (5, 18)
OUT_Z_C = (19, 20)


def _alibi_slopes(n):
    def pow2_slopes(m):
        start = 2.0 ** (-8.0 / m)
        return [start ** (i + 1) for i in range(m)]
    if math.log2(n).is_integer():
        s = pow2_slopes(n)
    else:
        c = 2 ** int(math.floor(math.log2(n)))
        s = pow2_slopes(c) + pow2_slopes(2 * c)[0::2][: n - c]
    return [float(v) for v in np.asarray(s, dtype=np.float32)]


ALIBI_SLOPES = _alibi_slopes(N_Q_HEADS_A)


def _sigmoid(x):
    return 0.5 * (1.0 + jnp.tanh(0.5 * x))


def _activation(x, kind):
    if kind == "plain":
        return x
    if kind == "silu":
        return x * _sigmoid(x)
    if kind == "gelu":
        return 0.5 * x * (1.0 + lax.erf(x * SQRT_HALF))
    if kind == "sigmoid":
        return _sigmoid(x)
    raise ValueError(kind)


def _dot(a, b):
    return jnp.dot(a, b, preferred_element_type=jnp.float32)


def _dot_nt(a, b):
    return lax.dot_general(a, b, (((1,), (1,)), ((), ())), preferred_element_type=jnp.float32)


def _rmsnorm_kernel(x_ref, g_ref, o_ref):
    x = x_ref[...]
    ms = jnp.mean(x * x, axis=-1, keepdims=True)
    o_ref[...] = ((x * lax.rsqrt(ms + EPS)) * g_ref[...]).astype(o_ref.dtype)


def _rmsnorm_bf16(x2d, gain, tm):
    t, d = x2d.shape
    return pl.pallas_call(
        _rmsnorm_kernel,
        grid=(t // tm,),
        in_specs=[pl.BlockSpec((tm, d), lambda i: (i, 0)),
                  pl.BlockSpec((1, d), lambda i: (0, 0))],
        out_specs=pl.BlockSpec((tm, d), lambda i: (i, 0)),
        out_shape=jax.ShapeDtypeStruct((t, d), jnp.bfloat16),
        compiler_params=pltpu.CompilerParams(dimension_semantics=("arbitrary",),
                                             vmem_limit_bytes=VMEM_LIMIT),
        name="rmsnorm_bf16",
    )(x2d, gain.reshape(1, d))


def _inproj_kernel(src_ref, pat_ref, h_ref, w0_ref, w1_ref, w2_ref, o_ref, wbuf, *, patterns, chunk):
    jp = pl.program_id(0)
    i = pl.program_id(1)
    tm = h_ref.shape[0]
    slice_rows = w0_ref.shape[0]
    nxt = jp % 2
    cur = (jp + 1) % 2

    n_iter = tm // (INPROJ_CHUNKS_PER_ITER * chunk)
    part_rows = slice_rows // n_iter

    def cast_next_weights(part):
        src_rows = pl.ds(pl.multiple_of(part * part_rows, part_rows), part_rows)
        dst_rows = pl.ds(pl.multiple_of(i * slice_rows + part * part_rows, part_rows), part_rows)
        for s, w_ref in enumerate((w0_ref, w1_ref, w2_ref)):
            wbuf[nxt, dst_rows, s * COL_BLOCK:(s + 1) * COL_BLOCK] = (
                w_ref[src_rows, :].astype(wbuf.dtype))

    @pl.when(jp == 0)
    def _():
        for part in range(n_iter):
            cast_next_weights(part)

    pat = pat_ref[jnp.maximum(jp - 1, 0)]
    for p, pattern in enumerate(patterns):
        @pl.when((jp > 0) & (pat == p))
        def _(pattern=pattern):
            def body(it, carry):
                cast_next_weights(it)
                for c in range(INPROJ_CHUNKS_PER_ITER):
                    start = pl.multiple_of((it * INPROJ_CHUNKS_PER_ITER + c) * chunk, chunk)
                    rows = pl.ds(start, chunk)
                    hc = h_ref[rows, :]
                    for s in range(3):
                        acc = _dot(hc, wbuf[cur, :, s * COL_BLOCK:(s + 1) * COL_BLOCK])
                        o_ref[rows, s * COL_BLOCK:(s + 1) * COL_BLOCK] = _activation(
                            acc, pattern[s]).astype(o_ref.dtype)
                return carry

            lax.fori_loop(0, n_iter, body, 0)


def _inproj(h, w_in, src_blocks, step_pattern, patterns, tm, chunk):
    t, d = h.shape
    n_steps = len(step_pattern)
    m_tiles = t // tm
    assert len(src_blocks) == 3 * n_steps and d % m_tiles == 0
    slice_rows = d // m_tiles

    def w_spec(s):
        def imap(jp, i, src_ref, pat_ref):
            return (i, src_ref[3 * jnp.minimum(jp, n_steps - 1) + s])
        return pl.BlockSpec((slice_rows, COL_BLOCK), imap)

    def h_map(jp, i, src_ref, pat_ref):
        return (jnp.where(jp == 0, 0, i), 0)

    def o_map(jp, i, src_ref, pat_ref):
        return (jnp.where(jp == 0, 0, i), jnp.maximum(jp - 1, 0))

    return pl.pallas_call(
        functools.partial(_inproj_kernel, patterns=patterns, chunk=chunk),
        grid_spec=pltpu.PrefetchScalarGridSpec(
            num_scalar_prefetch=2,
            grid=(n_steps + 1, m_tiles),
            in_specs=[pl.BlockSpec((tm, d), h_map), w_spec(0), w_spec(1), w_spec(2)],
            out_specs=pl.BlockSpec((tm, TRIPLE), o_map),
            scratch_shapes=[pltpu.VMEM((2, d, TRIPLE), jnp.bfloat16)],
        ),
        out_shape=jax.ShapeDtypeStruct((t, n_steps * TRIPLE), jnp.bfloat16),
        compiler_params=pltpu.CompilerParams(dimension_semantics=("arbitrary", "arbitrary"),
                                             vmem_limit_bytes=VMEM_LIMIT),
        name="inproj",
    )(jnp.asarray(src_blocks, jnp.int32), jnp.asarray(step_pattern, jnp.int32),
      h, w_in, w_in, w_in)


def _attn_kernel(sink_ref, q_ref, kp_ref, kc_ref, kn_ref, vp_ref, vc_ref, vn_ref, z_ref, *rest,
                 n_cast):
    w_refs = rest[:n_cast]
    o_ref = rest[n_cast]
    wo_refs = rest[n_cast + 1:2 * n_cast + 1]
    kbuf, vbuf, bias_ref = rest[2 * n_cast + 1:]
    tq = q_ref.shape[0]
    nblk = tq // BLOCK
    bi = pl.program_id(0)
    n = pl.program_id(1)
    nt = pl.num_programs(1)
    c1 = LOG2_E / math.sqrt(HEAD_DIM_A)

    if n_cast:
        step = bi * nt + n
        steps_per_weight = (pl.num_programs(0) * nt) // n_cast
        for k in range(n_cast):
            @pl.when((step >= k * steps_per_weight) & (step < (k + 1) * steps_per_weight))
            def _(k=k):
                wo_refs[k][...] = w_refs[k][...].astype(wo_refs[k].dtype)

    @pl.when((bi == 0) & (n == 0))
    def _():
        row = lax.broadcasted_iota(jnp.int32, (BLOCK, 3 * BLOCK), 0)
        col = lax.broadcasted_iota(jnp.int32, (BLOCK, 3 * BLOCK), 1)
        dist = jnp.abs(row - (col - BLOCK))
        valid = dist <= BLOCK
        distf = dist.astype(jnp.float32)
        for h in range(N_Q_HEADS_A):
            hk, r = divmod(h, REP_A)
            bias_ref[hk, r * BLOCK:(r + 1) * BLOCK, :] = jnp.where(
                valid, (-ALIBI_SLOPES[h] * LOG2_E) * distf, NEG_INF)
        for hk in range(N_KV_HEADS_A):
            vbuf[:, (2 * hk + 1) * HEAD_DIM_A:(2 * hk + 2) * HEAD_DIM_A] = jnp.ones(
                (tq + 2 * BLOCK, HEAD_DIM_A), vbuf.dtype)

    kbuf[0:BLOCK, :] = kp_ref[...]
    kbuf[BLOCK:BLOCK + tq, :] = kc_ref[...]
    kbuf[BLOCK + tq:, :] = kn_ref[...]
    for hk in range(N_KV_HEADS_A):
        src = slice(hk * HEAD_DIM_A, (hk + 1) * HEAD_DIM_A)
        dst = slice(2 * hk * HEAD_DIM_A, (2 * hk + 1) * HEAD_DIM_A)
        vbuf[0:BLOCK, dst] = vp_ref[:, src]
        vbuf[BLOCK:BLOCK + tq, dst] = vc_ref[:, src]
        vbuf[BLOCK + tq:, dst] = vn_ref[:, src]

    edge_p = jnp.where(n == 0, NEG_INF, 0.0)
    edge_n = jnp.where(n == nt - 1, NEG_INF, 0.0)

    units = [(blk, hk) for blk in range(nblk) for hk in range(N_KV_HEADS_A)]

    def scores(unit):
        blk, hk = unit
        rows = slice(blk * BLOCK, (blk + 1) * BLOCK)
        q = jnp.concatenate(
            [q_ref[rows, (hk * REP_A + r) * HEAD_DIM_A:(hk * REP_A + r + 1) * HEAD_DIM_A]
             for r in range(REP_A)], axis=0)
        k = kbuf[blk * BLOCK:(blk + 3) * BLOCK, hk * HEAD_DIM_A:(hk + 1) * HEAD_DIM_A]
        return _dot_nt(q, k)

    s_next = scores(units[0])
    for i, (blk, hk) in enumerate(units):
        s = s_next
        if i + 1 < len(units):
            s_next = scores(units[i + 1])
        rows = slice(blk * BLOCK, (blk + 1) * BLOCK)
        krows = slice(blk * BLOCK, (blk + 3) * BLOCK)
        t = s * c1 + bias_ref[hk]
        t_p, t_c, t_n = t[:, 0:BLOCK], t[:, BLOCK:2 * BLOCK], t[:, 2 * BLOCK:]
        if blk == 0:
            t_p = t_p + edge_p
        if blk == nblk - 1:
            t_n = t_n + edge_n
        sink2 = jnp.concatenate(
            [jnp.full((BLOCK, BLOCK), sink_ref[hk * REP_A + r] * LOG2_E, jnp.float32)
             for r in range(REP_A)], axis=0)
        m = jnp.max(jnp.maximum(jnp.maximum(t_p, t_c), t_n), axis=-1, keepdims=True)
        m = jnp.maximum(jnp.broadcast_to(m, (REP_A * BLOCK, BLOCK)), sink2)
        p = jnp.concatenate([jnp.exp2(t_p - m), jnp.exp2(t_c - m), jnp.exp2(t_n - m)],
                            axis=1).astype(jnp.bfloat16)
        ov = _dot(p, vbuf[krows, 2 * hk * HEAD_DIM_A:(2 * hk + 2) * HEAD_DIM_A])
        denom = ov[:, HEAD_DIM_A:] + jnp.exp2(sink2 - m)
        o = ov[:, :HEAD_DIM_A] * (1.0 / denom)
        for r in range(REP_A):
            hsl = slice((hk * REP_A + r) * HEAD_DIM_A, (hk * REP_A + r + 1) * HEAD_DIM_A)
            z = z_ref[rows, hsl].astype(jnp.float32)
            o_ref[rows, hsl] = (o[r * BLOCK:(r + 1) * BLOCK] * z).astype(o_ref.dtype)


def _branch_a(proj3, sink, tq, cast_weights):
    b, s, _ = proj3.shape
    nb = s // BLOCK
    per = tq // BLOCK
    nt = s // tq
    n_cast = len(cast_weights)
    steps_per_weight = (b * nt) // n_cast
    assert steps_per_weight * n_cast == b * nt

    def cast_spec(k, w):
        rows, cols = w.shape
        assert rows % steps_per_weight == 0
        def imap(bi, n, sink_ref):
            return (jnp.clip(bi * nt + n - k * steps_per_weight, 0, steps_per_weight - 1), 0)
        return pl.BlockSpec((rows // steps_per_weight, cols), imap)

    cast_specs = [cast_spec(k, w) for k, w in enumerate(cast_weights)]

    def tile(width, idx):
        return pl.BlockSpec((None, tq, width), lambda bi, n, sink_ref: (bi, n, idx))

    def halo(idx, after):
        def imap(bi, n, sink_ref):
            blk = (n + 1) * per if after else n * per - 1
            return (bi, jnp.clip(blk, 0, nb - 1), idx)
        return pl.BlockSpec((None, BLOCK, KV_WIDTH_A), imap)

    in_specs = [
        tile(WIDTH_A, OUT_Q_A_T),
        halo(OUT_K_A, False), tile(KV_WIDTH_A, OUT_K_A), halo(OUT_K_A, True),
        halo(OUT_V_A, False), tile(KV_WIDTH_A, OUT_V_A), halo(OUT_V_A, True),
        tile(WIDTH_A, OUT_Z_A_T),
    ] + cast_specs
    outs = pl.pallas_call(
        functools.partial(_attn_kernel, n_cast=n_cast),
        grid_spec=pltpu.PrefetchScalarGridSpec(
            num_scalar_prefetch=1,
            grid=(b, nt),
            in_specs=in_specs,
            out_specs=[pl.BlockSpec((None, tq, WIDTH_A), lambda bi, n, sink_ref: (bi, n, 0))]
            + cast_specs,
            scratch_shapes=[pltpu.VMEM((tq + 2 * BLOCK, KV_WIDTH_A), jnp.bfloat16),
                            pltpu.VMEM((tq + 2 * BLOCK, 2 * KV_WIDTH_A), jnp.bfloat16),
                            pltpu.VMEM((N_KV_HEADS_A, REP_A * BLOCK, 3 * BLOCK), jnp.float32)],
        ),
        out_shape=[jax.ShapeDtypeStruct((b, s, WIDTH_A), jnp.bfloat16)]
        + [jax.ShapeDtypeStruct(w.shape, jnp.bfloat16) for w in cast_weights],
        compiler_params=pltpu.CompilerParams(dimension_semantics=("arbitrary", "arbitrary"),
                                             vmem_limit_bytes=VMEM_LIMIT),
        name="branch_a_attn",
    )(sink, *([proj3] * 8), *cast_weights)
    return outs[0], outs[1:]


def _gmlp_tile(u_ref, v_ref, z_ref, lng_ref, lnb_ref, ws_ref, bs_ref, o_ref):
    tm = o_ref.shape[0]
    nc = tm // CHUNK
    gv = v_ref[...].astype(jnp.float32)
    mu = jnp.mean(gv, axis=-1, keepdims=True)
    xc = gv - mu
    var = jnp.mean(xc * xc, axis=-1, keepdims=True)
    vn = (xc * lax.rsqrt(var + EPS) * lng_ref[...] + lnb_ref[...]).astype(jnp.bfloat16)
    for g in range(N_GROUPS_B):
        gsl = slice(g * CHUNK, (g + 1) * CHUNK)
        rhs = jnp.concatenate([vn[c * CHUNK:(c + 1) * CHUNK, gsl] for c in range(nc)], axis=1)
        sg = _dot(ws_ref[g], rhs)
        bias = bs_ref[g]
        for c in range(nc):
            rows = slice(c * CHUNK, (c + 1) * CHUNK)
            s_c = sg[:, c * CHUNK:(c + 1) * CHUNK] + bias
            u = u_ref[rows, gsl].astype(jnp.float32)
            z = z_ref[rows, gsl].astype(jnp.float32)
            o_ref[rows, gsl] = (u * s_c * z).astype(o_ref.dtype)


def _mem_kv_kernel(mem_ref, g_ref, w_ref, o_ref, memh_s):
    @pl.when(pl.program_id(0) == 0)
    def _():
        x = mem_ref[...]
        ms = jnp.mean(x * x, axis=-1, keepdims=True)
        memh_s[...] = ((x * lax.rsqrt(ms + EPS)) * g_ref[...]).astype(memh_s.dtype)

    o_ref[...] = _dot(memh_s[...], w_ref[...].astype(jnp.bfloat16)).astype(o_ref.dtype)


def _mem_kv(mem2d, gain, w_kv, tn):
    m, d = mem2d.shape
    n = w_kv.shape[1]
    return pl.pallas_call(
        _mem_kv_kernel,
        grid=(n // tn,),
        in_specs=[pl.BlockSpec((m, d), lambda j: (0, 0), pipeline_mode=pl.Buffered(1)),
                  pl.BlockSpec((1, d), lambda j: (0, 0)),
                  pl.BlockSpec((d, tn), lambda j: (0, j))],
        out_specs=pl.BlockSpec((m, tn), lambda j: (0, j)),
        out_shape=jax.ShapeDtypeStruct((m, n), jnp.bfloat16),
        scratch_shapes=[pltpu.VMEM((m, d), jnp.bfloat16)],
        compiler_params=pltpu.CompilerParams(dimension_semantics=("arbitrary",),
                                             vmem_limit_bytes=VMEM_LIMIT),
        name="mem_kv",
    )(mem2d, gain.reshape(1, d), w_kv)


def _xattn_tile(q0, q1, z0, z1, kv_ref, o_ref):
    scale = 1.0 / math.sqrt(HEAD_DIM_C)
    q_refs = (q0, q1)
    z_refs = (z0, z1)
    for h in range(N_HEADS_C):
        cb, off = divmod(h * HEAD_DIM_C, COL_BLOCK)
        q = q_refs[cb][:, off:off + HEAD_DIM_C]
        k = kv_ref[:, h * HEAD_DIM_C:(h + 1) * HEAD_DIM_C]
        v = kv_ref[:, WIDTH_C + h * HEAD_DIM_C:WIDTH_C + (h + 1) * HEAD_DIM_C]
        s = _dot_nt(q, k) * scale
        m = jnp.max(s, axis=-1, keepdims=True)
        p = jnp.exp(s - m)
        denom = jnp.sum(p, axis=-1, keepdims=True)
        o = _dot(p.astype(jnp.bfloat16), v) / denom
        z = z_refs[cb][:, off:off + HEAD_DIM_C].astype(jnp.float32)
        o_ref[:, h * HEAD_DIM_C:(h + 1) * HEAD_DIM_C] = (o * z).astype(o_ref.dtype)


def _tail_kernel(x_ref, ga_ref, u_ref, v_ref, zb_ref, qc0_ref, qc1_ref, zc0_ref, zc1_ref, kv_ref,
                 g0_ref, g1_ref, g2_ref, g3_ref, lng_ref, lnb_ref, ws_ref, bs_ref,
                 wa_ref, wb_ref, wc_ref, wo_ref, fg_ref, o_ref, gb_s, gc_s, *, final_norm):
    g_refs = (g0_ref, g1_ref, g2_ref, g3_ref)

    def gate(branch):
        parts = []
        for cb in range(branch * D_MODEL // COL_BLOCK, (branch + 1) * D_MODEL // COL_BLOCK):
            blk, off = divmod(cb * COL_BLOCK, TRIPLE)
            parts.append(g_refs[blk][:, off:off + COL_BLOCK])
        return jnp.concatenate(parts, axis=1).astype(jnp.float32)

    merged = _dot(ga_ref[...], wa_ref[...]) * gate(0)
    _gmlp_tile(u_ref, v_ref, zb_ref, lng_ref, lnb_ref, ws_ref, bs_ref, gb_s)
    merged = merged + _dot(gb_s[...], wb_ref[...]) * gate(1)
    _xattn_tile(qc0_ref, qc1_ref, zc0_ref, zc1_ref, kv_ref, gc_s)
    merged = merged + _dot(gc_s[...], wc_ref[...]) * gate(2)
    y = x_ref[...] + _dot(merged.astype(jnp.bfloat16), wo_ref[...])
    if final_norm:
        ms = jnp.mean(y * y, axis=-1, keepdims=True)
        y = (y * lax.rsqrt(ms + EPS)) * fg_ref[...]
    o_ref[...] = y


def _tail(x3, ga3, proj3, kv3, ln_gain, ln_bias, w_spatial_bf16, b_spatial_bcast,
          wa, wb, wc, wo, final_gain, tm, final_norm):
    b, s, d = x3.shape

    def rows(width, idx=0):
        return pl.BlockSpec((None, tm, width), lambda bi, i: (bi, i, idx))

    def resident(shape):
        zeros = (0,) * len(shape)
        return pl.BlockSpec(shape, lambda bi, i: zeros, pipeline_mode=pl.Buffered(1))

    in_specs = [
        rows(d), rows(WIDTH_A),
        rows(WIDTH_B, OUT_U_B_T), rows(WIDTH_B, OUT_V_B_T), rows(WIDTH_B, OUT_Z_B_T),
        rows(COL_BLOCK, OUT_Q_C[0]), rows(COL_BLOCK, OUT_Q_C[1]),
        rows(COL_BLOCK, OUT_Z_C[0]), rows(COL_BLOCK, OUT_Z_C[1]),
        pl.BlockSpec((None, MEM_LEN, 2 * WIDTH_C), lambda bi, i: (bi, 0, 0)),
        rows(TRIPLE, OUT_GATES_T[0]), rows(TRIPLE, OUT_GATES_T[1]),
        rows(TRIPLE, OUT_GATES_T[2]), rows(TRIPLE, OUT_GATES_T[3]),
        resident((1, WIDTH_B)), resident((1, WIDTH_B)),
        resident((N_GROUPS_B, CHUNK, CHUNK)), resident((N_GROUPS_B, CHUNK, CHUNK)),
        resident((WIDTH_A, d)), resident((WIDTH_B, d)), resident((WIDTH_C, d)), resident((d, d)),
        resident((1, d)),
    ]
    return pl.pallas_call(
        functools.partial(_tail_kernel, final_norm=final_norm),
        grid=(b, s // tm),
        in_specs=in_specs,
        out_specs=rows(d),
        out_shape=jax.ShapeDtypeStruct((b, s, d), jnp.float32),
        scratch_shapes=[pltpu.VMEM((tm, WIDTH_B), jnp.bfloat16),
                        pltpu.VMEM((tm, WIDTH_C), jnp.bfloat16)],
        compiler_params=pltpu.CompilerParams(dimension_semantics=("arbitrary", "arbitrary"),
                                             vmem_limit_bytes=VMEM_LIMIT),
        name="tail_merge_out",
    )(x3, ga3, *([proj3] * 7), kv3, *([proj3] * 4),
      ln_gain.reshape(1, WIDTH_B), ln_bias.reshape(1, WIDTH_B), w_spatial_bf16, b_spatial_bcast,
      wa, wb, wc, wo, final_gain.reshape(1, d))


def kernel(x, mem, norm_gain, mem_norm_gain, w_in, sink, ln_v_gain, ln_v_bias, w_spatial, b_spatial,
           w_kv_mem, w_br_a, w_br_b, w_br_c, w_out, final_gain):
    b, s, d = x.shape
    depth = w_in.shape[0]
    bf16 = jnp.bfloat16
    x2d = x.reshape(b * s, d)
    mem2d = mem.reshape(b * MEM_LEN, d)
    for l in range(depth):
        h = _rmsnorm_bf16(x2d, norm_gain[l], 512)
        proj = _inproj(h, w_in[l], PROJ_SRC_BLOCKS, PROJ_STEP_PATTERN, PROJ_PATTERNS, 2048, 512)
        proj3 = proj.reshape(b, s, N_PROJ)

        ga3, (wa, wb, wc, wo) = _branch_a(proj3, sink[l].astype(jnp.float32), 512,
                                          (w_br_a[l], w_br_b[l], w_br_c[l], w_out[l]))

        kv = _mem_kv(mem2d, mem_norm_gain[l], w_kv_mem[l], 512)

        bs_bcast = jnp.broadcast_to(b_spatial[l].astype(jnp.float32)[:, :, None],
                                    (N_GROUPS_B, CHUNK, CHUNK))
        x3 = _tail(x2d.reshape(b, s, d), ga3, proj3,
                   kv.reshape(b, MEM_LEN, 2 * WIDTH_C), ln_v_gain[l], ln_v_bias[l],
                   w_spatial[l].astype(bf16), bs_bcast, wa, wb, wc, wo,
                   final_gain, 256, final_norm=(l == depth - 1))
        x2d = x3.reshape(b * s, d)
    return x2d.reshape(b, s, d)
```

```python
import functools
import math

import jax
import jax.numpy as jnp
import numpy as np
from jax import lax
from jax.experimental import pallas as pl
from jax.experimental.pallas import tpu as pltpu

D_MODEL = 2048
MEM_LEN = 256
EPS = 1e-6
HEAD_DIM_A = 128
WIDTH_A = 1536
N_Q_HEADS_A = 12
N_KV_HEADS_A = 4
REP_A = N_Q_HEADS_A // N_KV_HEADS_A
KV_WIDTH_A = 512
BLOCK = 128
WIDTH_B = 1536
CHUNK = 128
N_GROUPS_B = 12
N_HEADS_C = 4
WIDTH_C = 1024
HEAD_DIM_C = 256
NEG_INF = -1e30
N_GATES = 3 * D_MODEL

SQRT_HALF = math.sqrt(0.5)
LOG2_E = math.log2(math.e)
VMEM_LIMIT = 56 * 1024 * 1024
TAIL_VMEM_LIMIT = 63 * 1024 * 1024
QK_LOG2_SCALE = LOG2_E / math.sqrt(HEAD_DIM_A)

COL_BLOCK = 512
TRIPLE = 3 * COL_BLOCK
PROJ_SRC_BLOCKS = (0, 1, 2,
                   3, 4, 17,
                   5, 6, 7,
                   14, 15, 16,
                   8, 9, 10,
                   11, 12, 13,
                   18, 19, 20
                   ) + tuple(range(21, 33))
PROJ_PATTERNS = (("plain",) * 3, ("silu",) * 3, ("gelu",) * 3, ("plain", "silu", "silu"),
                 ("sigmoid",) * 3, ("qscale",) * 3)
PROJ_STEP_PATTERN = (5, 0, 1, 1, 2, 2, 3, 4, 4, 4, 4)
N_PROJ = len(PROJ_SRC_BLOCKS) * COL_BLOCK
INPROJ_CHUNKS_PER_ITER = 2
ATTN_LOOKAHEAD = 1
OUT_Q_A_T, OUT_Z_A_T, OUT_Z_B_T, OUT_U_B_T, OUT_V_B_T = 0, 2, 3, 4, 5
OUT_GATES_T = (7, 8, 9, 10)
OUT_K_A, OUT_V_A = 3, 4
OUT_Q_C = (5, 18)
OUT_Z_C = (19, 20)


def _alibi_slopes(n):
    def pow2_slopes(m):
        start = 2.0 ** (-8.0 / m)
        return [start ** (i + 1) for i in range(m)]
    if math.log2(n).is_integer():
        s = pow2_slopes(n)
    else:
        c = 2 ** int(math.floor(math.log2(n)))
        s = pow2_slopes(c) + pow2_slopes(2 * c)[0::2][: n - c]
    return [float(v) for v in np.asarray(s, dtype=np.float32)]


ALIBI_SLOPES = _alibi_slopes(N_Q_HEADS_A)


def _sigmoid(x):
    return 0.5 * (1.0 + jnp.tanh(0.5 * x))


def _activation(x, kind):
    if kind == "plain":
        return x
    if kind == "qscale":
        return x * QK_LOG2_SCALE
    if kind == "silu":
        return x * _sigmoid(x)
    if kind == "gelu":
        return 0.5 * x * (1.0 + lax.erf(x * SQRT_HALF))
    if kind == "sigmoid":
        return _sigmoid(x)
    raise ValueError(kind)


def _dot(a, b):
    return jnp.dot(a, b, preferred_element_type=jnp.float32)


def _dot_nt(a, b):
    return lax.dot_general(a, b, (((1,), (1,)), ((), ())), preferred_element_type=jnp.float32)


def _rmsnorm_kernel(x_ref, g_ref, o_ref):
    x = x_ref[...]
    ms = jnp.mean(x * x, axis=-1, keepdims=True)
    o_ref[...] = ((x * lax.rsqrt(ms + EPS)) * g_ref[...]).astype(o_ref.dtype)


def _rmsnorm_bf16(x2d, gain, tm):
    t, d = x2d.shape
    return pl.pallas_call(
        _rmsnorm_kernel,
        grid=(t // tm,),
        in_specs=[pl.BlockSpec((tm, d), lambda i: (i, 0)),
                  pl.BlockSpec((1, d), lambda i: (0, 0))],
        out_specs=pl.BlockSpec((tm, d), lambda i: (i, 0)),
        out_shape=jax.ShapeDtypeStruct((t, d), jnp.bfloat16),
        compiler_params=pltpu.CompilerParams(dimension_semantics=("arbitrary",),
                                             vmem_limit_bytes=VMEM_LIMIT),
        name="rmsnorm_bf16",
    )(x2d, gain.reshape(1, d))


def _inproj_kernel(src_ref, pat_ref, h_ref, w0_ref, w1_ref, w2_ref, o_ref, wbuf, *, patterns, chunk):
    jp = pl.program_id(0)
    i = pl.program_id(1)
    tm = h_ref.shape[0]
    slice_rows = w0_ref.shape[0]
    nxt = jp % 2
    cur = (jp + 1) % 2

    n_iter = tm // (INPROJ_CHUNKS_PER_ITER * chunk)
    part_rows = slice_rows // n_iter

    def cast_next_weights(part):
        src_rows = pl.ds(pl.multiple_of(part * part_rows, part_rows), part_rows)
        dst_rows = pl.ds(pl.multiple_of(i * slice_rows + part * part_rows, part_rows), part_rows)
        for s, w_ref in enumerate((w0_ref, w1_ref, w2_ref)):
            wbuf[nxt, dst_rows, s * COL_BLOCK:(s + 1) * COL_BLOCK] = (
                w_ref[src_rows, :].astype(wbuf.dtype))

    @pl.when(jp == 0)
    def _():
        for part in range(n_iter):
            cast_next_weights(part)

    pat = pat_ref[jnp.maximum(jp - 1, 0)]
    for p, pattern in enumerate(patterns):
        @pl.when((jp > 0) & (pat == p))
        def _(pattern=pattern):
            def body(it, carry):
                cast_next_weights(it)
                for c in range(INPROJ_CHUNKS_PER_ITER):
                    start = pl.multiple_of((it * INPROJ_CHUNKS_PER_ITER + c) * chunk, chunk)
                    rows = pl.ds(start, chunk)
                    hc = h_ref[rows, :]
                    for s in range(3):
                        acc = _dot(hc, wbuf[cur, :, s * COL_BLOCK:(s + 1) * COL_BLOCK])
                        o_ref[rows, s * COL_BLOCK:(s + 1) * COL_BLOCK] = _activation(
                            acc, pattern[s]).astype(o_ref.dtype)
                return carry

            lax.fori_loop(0, n_iter, body, 0)


def _inproj(h, w_in, src_blocks, step_pattern, patterns, tm, chunk):
    t, d = h.shape
    n_steps = len(step_pattern)
    m_tiles = t // tm
    assert len(src_blocks) == 3 * n_steps and d % m_tiles == 0
    slice_rows = d // m_tiles

    def w_spec(s):
        def imap(jp, i, src_ref, pat_ref):
            return (i, src_ref[3 * jnp.minimum(jp, n_steps - 1) + s])
        return pl.BlockSpec((slice_rows, COL_BLOCK), imap)

    def h_map(jp, i, src_ref, pat_ref):
        return (jnp.where(jp == 0, 0, i), 0)

    def o_map(jp, i, src_ref, pat_ref):
        return (jnp.where(jp == 0, 0, i), jnp.maximum(jp - 1, 0))

    return pl.pallas_call(
        functools.partial(_inproj_kernel, patterns=patterns, chunk=chunk),
        grid_spec=pltpu.PrefetchScalarGridSpec(
            num_scalar_prefetch=2,
            grid=(n_steps + 1, m_tiles),
            in_specs=[pl.BlockSpec((tm, d), h_map), w_spec(0), w_spec(1), w_spec(2)],
            out_specs=pl.BlockSpec((tm, TRIPLE), o_map),
            scratch_shapes=[pltpu.VMEM((2, d, TRIPLE), jnp.bfloat16)],
        ),
        out_shape=jax.ShapeDtypeStruct((t, n_steps * TRIPLE), jnp.bfloat16),
        compiler_params=pltpu.CompilerParams(dimension_semantics=("arbitrary", "arbitrary"),
                                             vmem_limit_bytes=VMEM_LIMIT),
        name="inproj",
    )(jnp.asarray(src_blocks, jnp.int32), jnp.asarray(step_pattern, jnp.int32),
      h, w_in, w_in, w_in)


def _attn_kernel(sink_ref, q_ref, kp_ref, kc_ref, kn_ref, vp_ref, vc_ref, vn_ref, z_ref, *rest,
                 n_cast):
    w_refs = rest[:n_cast]
    o_ref = rest[n_cast]
    wo_refs = rest[n_cast + 1:2 * n_cast + 1]
    kbuf, vbuf, bias_ref = rest[2 * n_cast + 1:]
    tq = q_ref.shape[0]
    nblk = tq // BLOCK
    bi = pl.program_id(0)
    n = pl.program_id(1)
    nt = pl.num_programs(1)

    if n_cast:
        step = bi * nt + n
        steps_per_weight = (pl.num_programs(0) * nt) // n_cast
        for k in range(n_cast):
            @pl.when((step >= k * steps_per_weight) & (step < (k + 1) * steps_per_weight))
            def _(k=k):
                wo_refs[k][...] = w_refs[k][...].astype(wo_refs[k].dtype)

    @pl.when((bi == 0) & (n == 0))
    def _():
        row = lax.broadcasted_iota(jnp.int32, (BLOCK, 3 * BLOCK), 0)
        col = lax.broadcasted_iota(jnp.int32, (BLOCK, 3 * BLOCK), 1)
        dist = jnp.abs(row - (col - BLOCK))
        valid = dist <= BLOCK
        distf = dist.astype(jnp.float32)
        for h in range(N_Q_HEADS_A):
            hk, r = divmod(h, REP_A)
            bias_ref[hk, r * BLOCK:(r + 1) * BLOCK, :] = jnp.where(
                valid, (-ALIBI_SLOPES[h] * LOG2_E) * distf, NEG_INF)
        for hk in range(N_KV_HEADS_A):
            vbuf[:, (2 * hk + 1) * HEAD_DIM_A:(2 * hk + 2) * HEAD_DIM_A] = jnp.ones(
                (tq + 2 * BLOCK, HEAD_DIM_A), vbuf.dtype)

    kbuf[0:BLOCK, :] = kp_ref[...]
    kbuf[BLOCK:BLOCK + tq, :] = kc_ref[...]
    kbuf[BLOCK + tq:, :] = kn_ref[...]
    for hk in range(N_KV_HEADS_A):
        src = slice(hk * HEAD_DIM_A, (hk + 1) * HEAD_DIM_A)
        dst = slice(2 * hk * HEAD_DIM_A, (2 * hk + 1) * HEAD_DIM_A)
        vbuf[0:BLOCK, dst] = vp_ref[:, src]
        vbuf[BLOCK:BLOCK + tq, dst] = vc_ref[:, src]
        vbuf[BLOCK + tq:, dst] = vn_ref[:, src]

    edge_p = jnp.where(n == 0, NEG_INF, 0.0)
    edge_n = jnp.where(n == nt - 1, NEG_INF, 0.0)

    units = [(blk, hk) for blk in range(nblk) for hk in range(N_KV_HEADS_A)]

    def scores(unit):
        blk, hk = unit
        rows = slice(blk * BLOCK, (blk + 1) * BLOCK)
        q = jnp.concatenate(
            [q_ref[rows, (hk * REP_A + r) * HEAD_DIM_A:(hk * REP_A + r + 1) * HEAD_DIM_A]
             for r in range(REP_A)], axis=0)
        k = kbuf[blk * BLOCK:(blk + 3) * BLOCK, hk * HEAD_DIM_A:(hk + 1) * HEAD_DIM_A]
        return _dot_nt(q, k)

    pending = [scores(u) for u in units[:ATTN_LOOKAHEAD]]
    for i, (blk, hk) in enumerate(units):
        s = pending.pop(0)
        if i + ATTN_LOOKAHEAD < len(units):
            pending.append(scores(units[i + ATTN_LOOKAHEAD]))
        rows = slice(blk * BLOCK, (blk + 1) * BLOCK)
        krows = slice(blk * BLOCK, (blk + 3) * BLOCK)
        t = s + bias_ref[hk]
        t_p, t_c, t_n = t[:, 0:BLOCK], t[:, BLOCK:2 * BLOCK], t[:, 2 * BLOCK:]
        if blk == 0:
            t_p = t_p + edge_p
        if blk == nblk - 1:
            t_n = t_n + edge_n
        sink2 = jnp.concatenate(
            [jnp.full((BLOCK, BLOCK), sink_ref[hk * REP_A + r] * LOG2_E, jnp.float32)
             for r in range(REP_A)], axis=0)
        m = jnp.max(jnp.maximum(jnp.maximum(t_p, t_c), t_n), axis=-1, keepdims=True)
        m = jnp.maximum(jnp.broadcast_to(m, (REP_A * BLOCK, BLOCK)), sink2)
        p = jnp.concatenate([jnp.exp2(t_p - m), jnp.exp2(t_c - m), jnp.exp2(t_n - m)],
                            axis=1).astype(jnp.bfloat16)
        ov = _dot(p, vbuf[krows, 2 * hk * HEAD_DIM_A:(2 * hk + 2) * HEAD_DIM_A])
        denom = ov[:, HEAD_DIM_A:] + jnp.exp2(sink2 - m)
        o = ov[:, :HEAD_DIM_A] * (1.0 / denom)
        for r in range(REP_A):
            hsl = slice((hk * REP_A + r) * HEAD_DIM_A, (hk * REP_A + r + 1) * HEAD_DIM_A)
            z = z_ref[rows, hsl].astype(jnp.float32)
            o_ref[rows, hsl] = (o[r * BLOCK:(r + 1) * BLOCK] * z).astype(o_ref.dtype)


def _branch_a(proj3, sink, tq, cast_weights):
    b, s, _ = proj3.shape
    nb = s // BLOCK
    per = tq // BLOCK
    nt = s // tq
    n_cast = len(cast_weights)
    steps_per_weight = (b * nt) // n_cast
    assert steps_per_weight * n_cast == b * nt

    def cast_spec(k, w):
        rows, cols = w.shape
        assert rows % steps_per_weight == 0
        def imap(bi, n, sink_ref):
            return (jnp.clip(bi * nt + n - k * steps_per_weight, 0, steps_per_weight - 1), 0)
        return pl.BlockSpec((rows // steps_per_weight, cols), imap)

    cast_specs = [cast_spec(k, w) for k, w in enumerate(cast_weights)]

    def tile(width, idx):
        return pl.BlockSpec((None, tq, width), lambda bi, n, sink_ref: (bi, n, idx))

    def halo(idx, after):
        def imap(bi, n, sink_ref):
            blk = (n + 1) * per if after else n * per - 1
            return (bi, jnp.clip(blk, 0, nb - 1), idx)
        return pl.BlockSpec((None, BLOCK, KV_WIDTH_A), imap)

    in_specs = [
        tile(WIDTH_A, OUT_Q_A_T),
        halo(OUT_K_A, False), tile(KV_WIDTH_A, OUT_K_A), halo(OUT_K_A, True),
        halo(OUT_V_A, False), tile(KV_WIDTH_A, OUT_V_A), halo(OUT_V_A, True),
        tile(WIDTH_A, OUT_Z_A_T),
    ] + cast_specs
    outs = pl.pallas_call(
        functools.partial(_attn_kernel, n_cast=n_cast),
        grid_spec=pltpu.PrefetchScalarGridSpec(
            num_scalar_prefetch=1,
            grid=(b, nt),
            in_specs=in_specs,
            out_specs=[pl.BlockSpec((None, tq, WIDTH_A), lambda bi, n, sink_ref: (bi, n, 0))]
            + cast_specs,
            scratch_shapes=[pltpu.VMEM((tq + 2 * BLOCK, KV_WIDTH_A), jnp.bfloat16),
                            pltpu.VMEM((tq + 2 * BLOCK, 2 * KV_WIDTH_A), jnp.bfloat16),
                            pltpu.VMEM((N_KV_HEADS_A, REP_A * BLOCK, 3 * BLOCK), jnp.float32)],
        ),
        out_shape=[jax.ShapeDtypeStruct((b, s, WIDTH_A), jnp.bfloat16)]
        + [jax.ShapeDtypeStruct(w.shape, jnp.bfloat16) for w in cast_weights],
        compiler_params=pltpu.CompilerParams(dimension_semantics=("arbitrary", "arbitrary"),
                                             vmem_limit_bytes=VMEM_LIMIT),
        name="branch_a_attn",
    )(sink, *([proj3] * 8), *cast_weights)
    return outs[0], outs[1:]


def _gmlp_spatial(v_ref, lng_ref, lnb_ref, ws_ref):
    nc = v_ref.shape[0] // CHUNK
    gv = v_ref[...].astype(jnp.float32)
    mu = jnp.mean(gv, axis=-1, keepdims=True)
    xc = gv - mu
    var = jnp.mean(xc * xc, axis=-1, keepdims=True)
    vn = (xc * lax.rsqrt(var + EPS) * lng_ref[...] + lnb_ref[...]).astype(jnp.bfloat16)
    out = []
    for g in range(N_GROUPS_B):
        gsl = slice(g * CHUNK, (g + 1) * CHUNK)
        rhs = jnp.concatenate([vn[c * CHUNK:(c + 1) * CHUNK, gsl] for c in range(nc)], axis=1)
        out.append(_dot(ws_ref[g], rhs))
    return out


def _gmlp_gate(spatial, u_ref, z_ref, bs_ref, o_ref):
    nc = o_ref.shape[0] // CHUNK
    for g, sg in enumerate(spatial):
        gsl = slice(g * CHUNK, (g + 1) * CHUNK)
        bias = bs_ref[g]
        for c in range(nc):
            rows = slice(c * CHUNK, (c + 1) * CHUNK)
            s_c = sg[:, c * CHUNK:(c + 1) * CHUNK] + bias
            u = u_ref[rows, gsl].astype(jnp.float32)
            z = z_ref[rows, gsl].astype(jnp.float32)
            o_ref[rows, gsl] = (u * s_c * z).astype(o_ref.dtype)


def _mem_kv_kernel(mem_ref, g_ref, w_ref, o_ref, memh_s):
    @pl.when(pl.program_id(0) == 0)
    def _():
        x = mem_ref[...]
        ms = jnp.mean(x * x, axis=-1, keepdims=True)
        memh_s[...] = ((x * lax.rsqrt(ms + EPS)) * g_ref[...]).astype(memh_s.dtype)

    o_ref[...] = _dot(memh_s[...], w_ref[...].astype(jnp.bfloat16)).astype(o_ref.dtype)


def _mem_kv(mem2d, gain, w_kv, tn):
    m, d = mem2d.shape
    n = w_kv.shape[1]
    return pl.pallas_call(
        _mem_kv_kernel,
        grid=(n // tn,),
        in_specs=[pl.BlockSpec((m, d), lambda j: (0, 0), pipeline_mode=pl.Buffered(1)),
                  pl.BlockSpec((1, d), lambda j: (0, 0)),
                  pl.BlockSpec((d, tn), lambda j: (0, j))],
        out_specs=pl.BlockSpec((m, tn), lambda j: (0, j)),
        out_shape=jax.ShapeDtypeStruct((m, n), jnp.bfloat16),
        scratch_shapes=[pltpu.VMEM((m, d), jnp.bfloat16)],
        compiler_params=pltpu.CompilerParams(dimension_semantics=("arbitrary",),
                                             vmem_limit_bytes=VMEM_LIMIT),
        name="mem_kv",
    )(mem2d, gain.reshape(1, d), w_kv)


def _xattn_scores(q_refs, kv_ref):
    out = []
    for h in range(N_HEADS_C):
        cb, off = divmod(h * HEAD_DIM_C, COL_BLOCK)
        q = q_refs[cb][:, off:off + HEAD_DIM_C]
        out.append(_dot_nt(q, kv_ref[:, h * HEAD_DIM_C:(h + 1) * HEAD_DIM_C]))
    return out


def _xattn_values(scores, kv_ref):
    scale = 1.0 / math.sqrt(HEAD_DIM_C)
    out = []
    for h, s in enumerate(scores):
        s = s * scale
        m = jnp.max(s, axis=-1, keepdims=True)
        p = jnp.exp(s - m)
        denom = jnp.sum(p, axis=-1, keepdims=True)
        v = kv_ref[:, WIDTH_C + h * HEAD_DIM_C:WIDTH_C + (h + 1) * HEAD_DIM_C]
        out.append((_dot(p.astype(jnp.bfloat16), v), denom))
    return out


def _xattn_finish(values, z_refs, o_ref):
    for h, (o, denom) in enumerate(values):
        cb, off = divmod(h * HEAD_DIM_C, COL_BLOCK)
        z = z_refs[cb][:, off:off + HEAD_DIM_C].astype(jnp.float32)
        o_ref[:, h * HEAD_DIM_C:(h + 1) * HEAD_DIM_C] = ((o / denom) * z).astype(o_ref.dtype)


def _tail_kernel(x_ref, ga_ref, g0_ref, g1_ref, g2_ref, g3_ref,
                 u_ref, v_ref, zb_ref, qc0_ref, qc1_ref, zc0_ref, zc1_ref, kv_ref,
                 lng_ref, lnb_ref, ws_ref, bs_ref, wa_ref, wb_ref, wc_ref, wo_ref, fg_ref,
                 o_ref, gb_s, gc_s, *, final_norm):
    d = D_MODEL
    g_refs = (g0_ref, g1_ref, g2_ref, g3_ref)

    def gate(branch):
        parts = []
        for cb in range(branch * D_MODEL // COL_BLOCK, (branch + 1) * D_MODEL // COL_BLOCK):
            blk, off = divmod(cb * COL_BLOCK, TRIPLE)
            parts.append(g_refs[blk][:, off:off + COL_BLOCK])
        return jnp.concatenate(parts, axis=1).astype(jnp.float32)

    half = d // 2
    ga = ga_ref[...]
    xs = _xattn_scores((qc0_ref, qc1_ref), kv_ref)
    a_lo = _dot(ga, wa_ref[:, :half])
    spatial = _gmlp_spatial(v_ref, lng_ref, lnb_ref, ws_ref)
    a_hi = _dot(ga, wa_ref[:, half:])
    xv = _xattn_values(xs, kv_ref)
    merged = jnp.concatenate([a_lo, a_hi], axis=1) * gate(0)
    _gmlp_gate(spatial, u_ref, zb_ref, bs_ref, gb_s)
    merged = merged + _dot(gb_s[...], wb_ref[...]) * gate(1)
    _xattn_finish(xv, (zc0_ref, zc1_ref), gc_s)
    merged = merged + _dot(gc_s[...], wc_ref[...]) * gate(2)
    y = x_ref[...] + _dot(merged.astype(jnp.bfloat16), wo_ref[...])
    if final_norm:
        ms = jnp.mean(y * y, axis=-1, keepdims=True)
        y = (y * lax.rsqrt(ms + EPS)) * fg_ref[...]
    o_ref[...] = y


def _tail(x3, ga3, proj3, kv3, ln_gain, ln_bias, w_spatial_bf16, b_spatial_bcast,
          wa, wb, wc, wo, final_gain, tm, final_norm):
    b, s, d = x3.shape

    def rows(width, idx=0):
        return pl.BlockSpec((None, tm, width), lambda bi, i: (bi, i, idx))

    def resident(shape):
        zeros = (0,) * len(shape)
        return pl.BlockSpec(shape, lambda bi, i: zeros, pipeline_mode=pl.Buffered(1))

    in_specs = [
        rows(d), rows(WIDTH_A),
        rows(TRIPLE, OUT_GATES_T[0]), rows(TRIPLE, OUT_GATES_T[1]),
        rows(TRIPLE, OUT_GATES_T[2]), rows(TRIPLE, OUT_GATES_T[3]),
        rows(WIDTH_B, OUT_U_B_T), rows(WIDTH_B, OUT_V_B_T), rows(WIDTH_B, OUT_Z_B_T),
        rows(COL_BLOCK, OUT_Q_C[0]), rows(COL_BLOCK, OUT_Q_C[1]),
        rows(COL_BLOCK, OUT_Z_C[0]), rows(COL_BLOCK, OUT_Z_C[1]),
        pl.BlockSpec((None, MEM_LEN, 2 * WIDTH_C), lambda bi, i: (bi, 0, 0)),
        resident((1, WIDTH_B)), resident((1, WIDTH_B)),
        resident((N_GROUPS_B, CHUNK, CHUNK)), resident((N_GROUPS_B, CHUNK, CHUNK)),
        resident((WIDTH_A, d)), resident((WIDTH_B, d)), resident((WIDTH_C, d)), resident((d, d)),
        resident((1, d)),
    ]
    return pl.pallas_call(
        functools.partial(_tail_kernel, final_norm=final_norm),
        grid=(b, s // tm),
        in_specs=in_specs,
        out_specs=rows(d),
        out_shape=jax.ShapeDtypeStruct((b, s, d), jnp.float32),
        scratch_shapes=[pltpu.VMEM((tm, WIDTH_B), jnp.bfloat16),
                        pltpu.VMEM((tm, WIDTH_C), jnp.bfloat16)],
        compiler_params=pltpu.CompilerParams(dimension_semantics=("arbitrary", "arbitrary"),
                                             vmem_limit_bytes=TAIL_VMEM_LIMIT),
        name="tail_merge_out",
    )(x3, ga3, *([proj3] * 4), *([proj3] * 7), kv3,
      ln_gain.reshape(1, WIDTH_B), ln_bias.reshape(1, WIDTH_B), w_spatial_bf16, b_spatial_bcast,
      wa, wb, wc, wo, final_gain.reshape(1, d))


def kernel(x, mem, norm_gain, mem_norm_gain, w_in, sink, ln_v_gain, ln_v_bias, w_spatial, b_spatial,
           w_kv_mem, w_br_a, w_br_b, w_br_c, w_out, final_gain):
    b, s, d = x.shape
    depth = w_in.shape[0]
    bf16 = jnp.bfloat16
    x2d = x.reshape(b * s, d)
    mem2d = mem.reshape(b * MEM_LEN, d)
    for l in range(depth):
        h = _rmsnorm_bf16(x2d, norm_gain[l], 512)
        proj = _inproj(h, w_in[l], PROJ_SRC_BLOCKS, PROJ_STEP_PATTERN, PROJ_PATTERNS, 2048, 512)
        proj3 = proj.reshape(b, s, N_PROJ)

        ga3, (wa, wb, wc, wo) = _branch_a(proj3, sink[l].astype(jnp.float32), 512,
                                          (w_br_a[l], w_br_b[l], w_br_c[l], w_out[l]))

        kv = _mem_kv(mem2d, mem_norm_gain[l], w_kv_mem[l], 512)

        bs_bcast = jnp.broadcast_to(b_spatial[l].astype(jnp.float32)[:, :, None],
                                    (N_GROUPS_B, CHUNK, CHUNK))
        x3 = _tail(x2d.reshape(b, s, d), ga3, proj3,
                   kv.reshape(b, MEM_LEN, 2 * WIDTH_C), ln_v_gain[l], ln_v_bias[l],
                   w_spatial[l].astype(bf16), bs_bcast, wa, wb, wc, wo,
                   final_gain, 256, final_norm=(l == depth - 1))
        x2d = x3.reshape(b * s, d)
    return x2d.reshape(b, s, d)
```

```python
import functools
import math

import jax
import jax.numpy as jnp
import numpy as np
from jax import lax
from jax.experimental import pallas as pl
from jax.experimental.pallas import tpu as pltpu

D_MODEL = 2048
MEM_LEN = 256
EPS = 1e-6
HEAD_DIM_A = 128
WIDTH_A = 1536
N_Q_HEADS_A = 12
N_KV_HEADS_A = 4
REP_A = N_Q_HEADS_A // N_KV_HEADS_A
KV_WIDTH_A = 512
BLOCK = 128
WIDTH_B = 1536
CHUNK = 128
N_GROUPS_B = 12
N_HEADS_C = 4
WIDTH_C = 1024
HEAD_DIM_C = 256
NEG_INF = -1e30
N_GATES = 3 * D_MODEL

SQRT_HALF = math.sqrt(0.5)
LOG2_E = math.log2(math.e)
VMEM_LIMIT = 56 * 1024 * 1024
TAIL_VMEM_LIMIT = 63 * 1024 * 1024
QK_LOG2_SCALE = LOG2_E / math.sqrt(HEAD_DIM_A)

COL_BLOCK = 512
TRIPLE = 3 * COL_BLOCK
PROJ_SRC_BLOCKS = (0, 1, 2,
                   3, 4, 17,
                   5, 6, 7,
                   14, 15, 16,
                   8, 9, 10,
                   11, 12, 13,
                   18, 19, 20
                   ) + tuple(range(21, 33))
PROJ_PATTERNS = (("plain",) * 3, ("silu",) * 3, ("gelu",) * 3, ("plain", "silu", "silu"),
                 ("sigmoid",) * 3, ("qscale",) * 3)
PROJ_STEP_PATTERN = (5, 0, 1, 1, 2, 2, 3, 4, 4, 4, 4)
N_PROJ = len(PROJ_SRC_BLOCKS) * COL_BLOCK
INPROJ_CHUNKS_PER_ITER = 2
ATTN_LOOKAHEAD = 1
OUT_Q_A_T, OUT_Z_A_T, OUT_Z_B_T, OUT_U_B_T, OUT_V_B_T = 0, 2, 3, 4, 5
OUT_GATES_T = (7, 8, 9, 10)
OUT_K_A, OUT_V_A = 3, 4
OUT_Q_C = (5, 18)
OUT_Z_C = (19, 20)


def _alibi_slopes(n):
    def pow2_slopes(m):
        start = 2.0 ** (-8.0 / m)
        return [start ** (i + 1) for i in range(m)]
    if math.log2(n).is_integer():
        s = pow2_slopes(n)
    else:
        c = 2 ** int(math.floor(math.log2(n)))
        s = pow2_slopes(c) + pow2_slopes(2 * c)[0::2][: n - c]
    return [float(v) for v in np.asarray(s, dtype=np.float32)]


ALIBI_SLOPES = _alibi_slopes(N_Q_HEADS_A)


def _sigmoid(x):
    return 0.5 * (1.0 + jnp.tanh(0.5 * x))


def _activation(x, kind):
    if kind == "plain":
        return x
    if kind == "qscale":
        return x * QK_LOG2_SCALE
    if kind == "silu":
        return x * _sigmoid(x)
    if kind == "gelu":
        return 0.5 * x * (1.0 + lax.erf(x * SQRT_HALF))
    if kind == "sigmoid":
        return _sigmoid(x)
    raise ValueError(kind)


def _dot(a, b):
    return jnp.dot(a, b, preferred_element_type=jnp.float32)


def _dot_nt(a, b):
    return lax.dot_general(a, b, (((1,), (1,)), ((), ())), preferred_element_type=jnp.float32)


def _rmsnorm_kernel(x_ref, g_ref, o_ref):
    x = x_ref[...]
    ms = jnp.mean(x * x, axis=-1, keepdims=True)
    o_ref[...] = ((x * lax.rsqrt(ms + EPS)) * g_ref[...]).astype(o_ref.dtype)


def _rmsnorm_bf16(x2d, gain, tm):
    t, d = x2d.shape
    return pl.pallas_call(
        _rmsnorm_kernel,
        grid=(t // tm,),
        in_specs=[pl.BlockSpec((tm, d), lambda i: (i, 0)),
                  pl.BlockSpec((1, d), lambda i: (0, 0))],
        out_specs=pl.BlockSpec((tm, d), lambda i: (i, 0)),
        out_shape=jax.ShapeDtypeStruct((t, d), jnp.bfloat16),
        compiler_params=pltpu.CompilerParams(dimension_semantics=("arbitrary",),
                                             vmem_limit_bytes=VMEM_LIMIT),
        name="rmsnorm_bf16",
    )(x2d, gain.reshape(1, d))


def _inproj_kernel(src_ref, pat_ref, h_ref, w0_ref, w1_ref, w2_ref, o_ref, wbuf, *, patterns, chunk):
    jp = pl.program_id(0)
    i = pl.program_id(1)
    tm = h_ref.shape[0]
    slice_rows = w0_ref.shape[0]
    nxt = jp % 2
    cur = (jp + 1) % 2

    n_iter = tm // (INPROJ_CHUNKS_PER_ITER * chunk)
    part_rows = slice_rows // n_iter

    def cast_next_weights(part):
        src_rows = pl.ds(pl.multiple_of(part * part_rows, part_rows), part_rows)
        dst_rows = pl.ds(pl.multiple_of(i * slice_rows + part * part_rows, part_rows), part_rows)
        for s, w_ref in enumerate((w0_ref, w1_ref, w2_ref)):
            wbuf[nxt, dst_rows, s * COL_BLOCK:(s + 1) * COL_BLOCK] = (
                w_ref[src_rows, :].astype(wbuf.dtype))

    @pl.when(jp == 0)
    def _():
        for part in range(n_iter):
            cast_next_weights(part)

    pat = pat_ref[jnp.maximum(jp - 1, 0)]
    for p, pattern in enumerate(patterns):
        @pl.when((jp > 0) & (pat == p))
        def _(pattern=pattern):
            def body(it, carry):
                cast_next_weights(it)
                for c in range(INPROJ_CHUNKS_PER_ITER):
                    start = pl.multiple_of((it * INPROJ_CHUNKS_PER_ITER + c) * chunk, chunk)
                    rows = pl.ds(start, chunk)
                    hc = h_ref[rows, :]
                    for s in range(3):
                        acc = _dot(hc, wbuf[cur, :, s * COL_BLOCK:(s + 1) * COL_BLOCK])
                        o_ref[rows, s * COL_BLOCK:(s + 1) * COL_BLOCK] = _activation(
                            acc, pattern[s]).astype(o_ref.dtype)
                return carry

            lax.fori_loop(0, n_iter, body, 0)


def _inproj(h, w_in, src_blocks, step_pattern, patterns, tm, chunk):
    t, d = h.shape
    n_steps = len(step_pattern)
    m_tiles = t // tm
    assert len(src_blocks) == 3 * n_steps and d % m_tiles == 0
    slice_rows = d // m_tiles

    def w_spec(s):
        def imap(jp, i, src_ref, pat_ref):
            return (i, src_ref[3 * jnp.minimum(jp, n_steps - 1) + s])
        return pl.BlockSpec((slice_rows, COL_BLOCK), imap)

    def h_map(jp, i, src_ref, pat_ref):
        return (jnp.where(jp == 0, 0, i), 0)

    def o_map(jp, i, src_ref, pat_ref):
        return (jnp.where(jp == 0, 0, i), jnp.maximum(jp - 1, 0))

    return pl.pallas_call(
        functools.partial(_inproj_kernel, patterns=patterns, chunk=chunk),
        grid_spec=pltpu.PrefetchScalarGridSpec(
            num_scalar_prefetch=2,
            grid=(n_steps + 1, m_tiles),
            in_specs=[pl.BlockSpec((tm, d), h_map), w_spec(0), w_spec(1), w_spec(2)],
            out_specs=pl.BlockSpec((tm, TRIPLE), o_map),
            scratch_shapes=[pltpu.VMEM((2, d, TRIPLE), jnp.bfloat16)],
        ),
        out_shape=jax.ShapeDtypeStruct((t, n_steps * TRIPLE), jnp.bfloat16),
        compiler_params=pltpu.CompilerParams(dimension_semantics=("arbitrary", "arbitrary"),
                                             vmem_limit_bytes=VMEM_LIMIT),
        name="inproj",
    )(jnp.asarray(src_blocks, jnp.int32), jnp.asarray(step_pattern, jnp.int32),
      h, w_in, w_in, w_in)


def _attn_kernel(sink_ref, q_ref, kp_ref, kc_ref, kn_ref, vp_ref, vc_ref, vn_ref, z_ref, *rest,
                 n_cast):
    w_refs = rest[:n_cast]
    o_ref = rest[n_cast]
    wo_refs = rest[n_cast + 1:2 * n_cast + 1]
    kbuf, vbuf, bias_ref = rest[2 * n_cast + 1:]
    tq = q_ref.shape[0]
    nblk = tq // BLOCK
    bi = pl.program_id(0)
    n = pl.program_id(1)
    nt = pl.num_programs(1)

    if n_cast:
        step = bi * nt + n
        steps_per_weight = (pl.num_programs(0) * nt) // n_cast
        for k in range(n_cast):
            @pl.when((step >= k * steps_per_weight) & (step < (k + 1) * steps_per_weight))
            def _(k=k):
                wo_refs[k][...] = w_refs[k][...].astype(wo_refs[k].dtype)

    @pl.when((bi == 0) & (n == 0))
    def _():
        row = lax.broadcasted_iota(jnp.int32, (BLOCK, 3 * BLOCK), 0)
        col = lax.broadcasted_iota(jnp.int32, (BLOCK, 3 * BLOCK), 1)
        dist = jnp.abs(row - (col - BLOCK))
        valid = dist <= BLOCK
        distf = dist.astype(jnp.float32)
        for h in range(N_Q_HEADS_A):
            hk, r = divmod(h, REP_A)
            bias_ref[hk, r * BLOCK:(r + 1) * BLOCK, :] = jnp.where(
                valid, (-ALIBI_SLOPES[h] * LOG2_E) * distf, NEG_INF)
        for hk in range(N_KV_HEADS_A):
            vbuf[:, (2 * hk + 1) * HEAD_DIM_A:(2 * hk + 2) * HEAD_DIM_A] = jnp.ones(
                (tq + 2 * BLOCK, HEAD_DIM_A), vbuf.dtype)

    kbuf[0:BLOCK, :] = kp_ref[...]
    kbuf[BLOCK:BLOCK + tq, :] = kc_ref[...]
    kbuf[BLOCK + tq:, :] = kn_ref[...]
    for hk in range(N_KV_HEADS_A):
        src = slice(hk * HEAD_DIM_A, (hk + 1) * HEAD_DIM_A)
        dst = slice(2 * hk * HEAD_DIM_A, (2 * hk + 1) * HEAD_DIM_A)
        vbuf[0:BLOCK, dst] = vp_ref[:, src]
        vbuf[BLOCK:BLOCK + tq, dst] = vc_ref[:, src]
        vbuf[BLOCK + tq:, dst] = vn_ref[:, src]

    edge_p = jnp.where(n == 0, NEG_INF, 0.0)
    edge_n = jnp.where(n == nt - 1, NEG_INF, 0.0)

    units = [(blk, hk) for blk in range(nblk) for hk in range(N_KV_HEADS_A)]

    def scores(unit):
        blk, hk = unit
        rows = slice(blk * BLOCK, (blk + 1) * BLOCK)
        q = jnp.concatenate(
            [q_ref[rows, (hk * REP_A + r) * HEAD_DIM_A:(hk * REP_A + r + 1) * HEAD_DIM_A]
             for r in range(REP_A)], axis=0)
        k = kbuf[blk * BLOCK:(blk + 3) * BLOCK, hk * HEAD_DIM_A:(hk + 1) * HEAD_DIM_A]
        return _dot_nt(q, k)

    pending = [scores(u) for u in units[:ATTN_LOOKAHEAD]]
    for i, (blk, hk) in enumerate(units):
        s = pending.pop(0)
        if i + ATTN_LOOKAHEAD < len(units):
            pending.append(scores(units[i + ATTN_LOOKAHEAD]))
        rows = slice(blk * BLOCK, (blk + 1) * BLOCK)
        krows = slice(blk * BLOCK, (blk + 3) * BLOCK)
        t = bias_ref[hk] + s
        t_p, t_c, t_n = t[:, 0:BLOCK], t[:, BLOCK:2 * BLOCK], t[:, 2 * BLOCK:]
        if blk == 0:
            t_p = t_p + edge_p
        if blk == nblk - 1:
            t_n = t_n + edge_n
        sink2 = jnp.concatenate(
            [jnp.full((BLOCK, BLOCK), sink_ref[hk * REP_A + r] * LOG2_E, jnp.float32)
             for r in range(REP_A)], axis=0)
        m = jnp.max(jnp.maximum(jnp.maximum(t_p, t_c), t_n), axis=-1, keepdims=True)
        m = jnp.maximum(jnp.broadcast_to(m, (REP_A * BLOCK, BLOCK)), sink2)
        p = jnp.concatenate([jnp.exp2(t_p - m), jnp.exp2(t_c - m), jnp.exp2(t_n - m)],
                            axis=1).astype(jnp.bfloat16)
        ov = _dot(p, vbuf[krows, 2 * hk * HEAD_DIM_A:(2 * hk + 2) * HEAD_DIM_A])
        denom = ov[:, HEAD_DIM_A:] + jnp.exp2(sink2 - m)
        o = ov[:, :HEAD_DIM_A] * (1.0 / denom)
        for r in range(REP_A):
            hsl = slice((hk * REP_A + r) * HEAD_DIM_A, (hk * REP_A + r + 1) * HEAD_DIM_A)
            z = z_ref[rows, hsl].astype(jnp.float32)
            o_ref[rows, hsl] = (o[r * BLOCK:(r + 1) * BLOCK] * z).astype(o_ref.dtype)


def _branch_a(proj3, sink, tq, cast_weights):
    b, s, _ = proj3.shape
    nb = s // BLOCK
    per = tq // BLOCK
    nt = s // tq
    n_cast = len(cast_weights)
    steps_per_weight = (b * nt) // n_cast
    assert steps_per_weight * n_cast == b * nt

    def cast_spec(k, w):
        rows, cols = w.shape
        assert rows % steps_per_weight == 0
        def imap(bi, n, sink_ref):
            return (jnp.clip(bi * nt + n - k * steps_per_weight, 0, steps_per_weight - 1), 0)
        return pl.BlockSpec((rows // steps_per_weight, cols), imap)

    cast_specs = [cast_spec(k, w) for k, w in enumerate(cast_weights)]

    def tile(width, idx):
        return pl.BlockSpec((None, tq, width), lambda bi, n, sink_ref: (bi, n, idx))

    def halo(idx, after):
        def imap(bi, n, sink_ref):
            blk = (n + 1) * per if after else n * per - 1
            return (bi, jnp.clip(blk, 0, nb - 1), idx)
        return pl.BlockSpec((None, BLOCK, KV_WIDTH_A), imap)

    in_specs = [
        tile(WIDTH_A, OUT_Q_A_T),
        halo(OUT_K_A, False), tile(KV_WIDTH_A, OUT_K_A), halo(OUT_K_A, True),
        halo(OUT_V_A, False), tile(KV_WIDTH_A, OUT_V_A), halo(OUT_V_A, True),
        tile(WIDTH_A, OUT_Z_A_T),
    ] + cast_specs
    outs = pl.pallas_call(
        functools.partial(_attn_kernel, n_cast=n_cast),
        grid_spec=pltpu.PrefetchScalarGridSpec(
            num_scalar_prefetch=1,
            grid=(b, nt),
            in_specs=in_specs,
            out_specs=[pl.BlockSpec((None, tq, WIDTH_A), lambda bi, n, sink_ref: (bi, n, 0))]
            + cast_specs,
            scratch_shapes=[pltpu.VMEM((tq + 2 * BLOCK, KV_WIDTH_A), jnp.bfloat16),
                            pltpu.VMEM((tq + 2 * BLOCK, 2 * KV_WIDTH_A), jnp.bfloat16),
                            pltpu.VMEM((N_KV_HEADS_A, REP_A * BLOCK, 3 * BLOCK), jnp.float32)],
        ),
        out_shape=[jax.ShapeDtypeStruct((b, s, WIDTH_A), jnp.bfloat16)]
        + [jax.ShapeDtypeStruct(w.shape, jnp.bfloat16) for w in cast_weights],
        compiler_params=pltpu.CompilerParams(dimension_semantics=("arbitrary", "arbitrary"),
                                             vmem_limit_bytes=VMEM_LIMIT),
        name="branch_a_attn",
    )(sink, *([proj3] * 8), *cast_weights)
    return outs[0], outs[1:]


def _gmlp_spatial(v_ref, lng_ref, lnb_ref, ws_ref):
    nc = v_ref.shape[0] // CHUNK
    gv = v_ref[...].astype(jnp.float32)
    mu = jnp.mean(gv, axis=-1, keepdims=True)
    xc = gv - mu
    var = jnp.mean(xc * xc, axis=-1, keepdims=True)
    vn = (xc * lax.rsqrt(var + EPS) * lng_ref[...] + lnb_ref[...]).astype(jnp.bfloat16)
    out = []
    for g in range(N_GROUPS_B):
        gsl = slice(g * CHUNK, (g + 1) * CHUNK)
        rhs = jnp.concatenate([vn[c * CHUNK:(c + 1) * CHUNK, gsl] for c in range(nc)], axis=1)
        out.append(_dot(ws_ref[g], rhs))
    return out


def _gmlp_gate(spatial, u_ref, z_ref, bs_ref, o_ref):
    nc = o_ref.shape[0] // CHUNK
    for g, sg in enumerate(spatial):
        gsl = slice(g * CHUNK, (g + 1) * CHUNK)
        bias = bs_ref[g]
        for c in range(nc):
            rows = slice(c * CHUNK, (c + 1) * CHUNK)
            s_c = sg[:, c * CHUNK:(c + 1) * CHUNK] + bias
            u = u_ref[rows, gsl].astype(jnp.float32)
            z = z_ref[rows, gsl].astype(jnp.float32)
            o_ref[rows, gsl] = (u * s_c * z).astype(o_ref.dtype)


def _mem_kv_kernel(mem_ref, g_ref, w_ref, o_ref, memh_s):
    @pl.when(pl.program_id(0) == 0)
    def _():
        x = mem_ref[...]
        ms = jnp.mean(x * x, axis=-1, keepdims=True)
        memh_s[...] = ((x * lax.rsqrt(ms + EPS)) * g_ref[...]).astype(memh_s.dtype)

    o_ref[...] = _dot(memh_s[...], w_ref[...].astype(jnp.bfloat16)).astype(o_ref.dtype)


def _mem_kv(mem2d, gain, w_kv, tn):
    m, d = mem2d.shape
    n = w_kv.shape[1]
    return pl.pallas_call(
        _mem_kv_kernel,
        grid=(n // tn,),
        in_specs=[pl.BlockSpec((m, d), lambda j: (0, 0), pipeline_mode=pl.Buffered(1)),
                  pl.BlockSpec((1, d), lambda j: (0, 0)),
                  pl.BlockSpec((d, tn), lambda j: (0, j))],
        out_specs=pl.BlockSpec((m, tn), lambda j: (0, j)),
        out_shape=jax.ShapeDtypeStruct((m, n), jnp.bfloat16),
        scratch_shapes=[pltpu.VMEM((m, d), jnp.bfloat16)],
        compiler_params=pltpu.CompilerParams(dimension_semantics=("arbitrary",),
                                             vmem_limit_bytes=VMEM_LIMIT),
        name="mem_kv",
    )(mem2d, gain.reshape(1, d), w_kv)


def _xattn_scores(q_refs, kv_ref):
    out = []
    for h in range(N_HEADS_C):
        cb, off = divmod(h * HEAD_DIM_C, COL_BLOCK)
        q = q_refs[cb][:, off:off + HEAD_DIM_C]
        out.append(_dot_nt(q, kv_ref[:, h * HEAD_DIM_C:(h + 1) * HEAD_DIM_C]))
    return out


def _xattn_values(scores, kv_ref):
    scale = 1.0 / math.sqrt(HEAD_DIM_C)
    out = []
    for h, s in enumerate(scores):
        s = s * scale
        m = jnp.max(s, axis=-1, keepdims=True)
        p = jnp.exp(s - m)
        denom = jnp.sum(p, axis=-1, keepdims=True)
        v = kv_ref[:, WIDTH_C + h * HEAD_DIM_C:WIDTH_C + (h + 1) * HEAD_DIM_C]
        out.append((_dot(p.astype(jnp.bfloat16), v), denom))
    return out


def _xattn_finish(values, z_refs, o_ref):
    for h, (o, denom) in enumerate(values):
        cb, off = divmod(h * HEAD_DIM_C, COL_BLOCK)
        z = z_refs[cb][:, off:off + HEAD_DIM_C].astype(jnp.float32)
        o_ref[:, h * HEAD_DIM_C:(h + 1) * HEAD_DIM_C] = ((o / denom) * z).astype(o_ref.dtype)


def _tail_kernel(x_ref, ga_ref, g0_ref, g1_ref, g2_ref, g3_ref,
                 u_ref, v_ref, zb_ref, qc0_ref, qc1_ref, zc0_ref, zc1_ref, kv_ref,
                 lng_ref, lnb_ref, ws_ref, bs_ref, wa_ref, wb_ref, wc_ref, wo_ref, fg_ref,
                 o_ref, gb_s, gc_s, *, final_norm):
    d = D_MODEL
    g_refs = (g0_ref, g1_ref, g2_ref, g3_ref)

    def gate(branch):
        parts = []
        for cb in range(branch * D_MODEL // COL_BLOCK, (branch + 1) * D_MODEL // COL_BLOCK):
            blk, off = divmod(cb * COL_BLOCK, TRIPLE)
            parts.append(g_refs[blk][:, off:off + COL_BLOCK])
        return jnp.concatenate(parts, axis=1).astype(jnp.float32)

    half = d // 2
    ga = ga_ref[...]
    xs = _xattn_scores((qc0_ref, qc1_ref), kv_ref)
    a_lo = _dot(ga, wa_ref[:, :half])
    spatial = _gmlp_spatial(v_ref, lng_ref, lnb_ref, ws_ref)
    a_hi = _dot(ga, wa_ref[:, half:])
    xv = _xattn_values(xs, kv_ref)
    merged = jnp.concatenate([a_lo, a_hi], axis=1) * gate(0)
    _gmlp_gate(spatial, u_ref, zb_ref, bs_ref, gb_s)
    merged = merged + _dot(gb_s[...], wb_ref[...]) * gate(1)
    _xattn_finish(xv, (zc0_ref, zc1_ref), gc_s)
    merged = merged + _dot(gc_s[...], wc_ref[...]) * gate(2)
    y = x_ref[...] + _dot(merged.astype(jnp.bfloat16), wo_ref[...])
    if final_norm:
        ms = jnp.mean(y * y, axis=-1, keepdims=True)
        y = (y * lax.rsqrt(ms + EPS)) * fg_ref[...]
    o_ref[...] = y


def _tail(x3, ga3, proj3, kv3, ln_gain, ln_bias, w_spatial_bf16, b_spatial_bcast,
          wa, wb, wc, wo, final_gain, tm, final_norm):
    b, s, d = x3.shape

    def rows(width, idx=0):
        return pl.BlockSpec((None, tm, width), lambda bi, i: (bi, i, idx))

    def resident(shape):
        zeros = (0,) * len(shape)
        return pl.BlockSpec(shape, lambda bi, i: zeros, pipeline_mode=pl.Buffered(1))

    in_specs = [
        rows(d), rows(WIDTH_A),
        rows(TRIPLE, OUT_GATES_T[0]), rows(TRIPLE, OUT_GATES_T[1]),
        rows(TRIPLE, OUT_GATES_T[2]), rows(TRIPLE, OUT_GATES_T[3]),
        rows(WIDTH_B, OUT_U_B_T), rows(WIDTH_B, OUT_V_B_T), rows(WIDTH_B, OUT_Z_B_T),
        rows(COL_BLOCK, OUT_Q_C[0]), rows(COL_BLOCK, OUT_Q_C[1]),
        rows(COL_BLOCK, OUT_Z_C[0]), rows(COL_BLOCK, OUT_Z_C[1]),
        pl.BlockSpec((None, MEM_LEN, 2 * WIDTH_C), lambda bi, i: (bi, 0, 0)),
        resident((1, WIDTH_B)), resident((1, WIDTH_B)),
        resident((N_GROUPS_B, CHUNK, CHUNK)), resident((N_GROUPS_B, CHUNK, CHUNK)),
        resident((WIDTH_A, d)), resident((WIDTH_B, d)), resident((WIDTH_C, d)), resident((d, d)),
        resident((1, d)),
    ]
    return pl.pallas_call(
        functools.partial(_tail_kernel, final_norm=final_norm),
        grid=(b, s // tm),
        in_specs=in_specs,
        out_specs=rows(d),
        out_shape=jax.ShapeDtypeStruct((b, s, d), jnp.float32),
        scratch_shapes=[pltpu.VMEM((tm, WIDTH_B), jnp.bfloat16),
                        pltpu.VMEM((tm, WIDTH_C), jnp.bfloat16)],
        compiler_params=pltpu.CompilerParams(dimension_semantics=("arbitrary", "arbitrary"),
                                             vmem_limit_bytes=TAIL_VMEM_LIMIT),
        name="tail_merge_out",
    )(x3, ga3, *([proj3] * 4), *([proj3] * 7), kv3,
      ln_gain.reshape(1, WIDTH_B), ln_bias.reshape(1, WIDTH_B), w_spatial_bf16, b_spatial_bcast,
      wa, wb, wc, wo, final_gain.reshape(1, d))


def kernel(x, mem, norm_gain, mem_norm_gain, w_in, sink, ln_v_gain, ln_v_bias, w_spatial, b_spatial,
           w_kv_mem, w_br_a, w_br_b, w_br_c, w_out, final_gain):
    b, s, d = x.shape
    depth = w_in.shape[0]
    bf16 = jnp.bfloat16
    x2d = x.reshape(b * s, d)
    mem2d = mem.reshape(b * MEM_LEN, d)
    for l in range(depth):
        h = _rmsnorm_bf16(x2d, norm_gain[l], 1024)
        proj = _inproj(h, w_in[l], PROJ_SRC_BLOCKS, PROJ_STEP_PATTERN, PROJ_PATTERNS, 2048, 512)
        proj3 = proj.reshape(b, s, N_PROJ)

        ga3, (wa, wb, wc, wo) = _branch_a(proj3, sink[l].astype(jnp.float32), 512,
                                          (w_br_a[l], w_br_b[l], w_br_c[l], w_out[l]))

        kv = _mem_kv(mem2d, mem_norm_gain[l], w_kv_mem[l], 512)

        bs_bcast = jnp.broadcast_to(b_spatial[l].astype(jnp.float32)[:, :, None],
                                    (N_GROUPS_B, CHUNK, CHUNK))
        x3 = _tail(x2d.reshape(b, s, d), ga3, proj3,
                   kv.reshape(b, MEM_LEN, 2 * WIDTH_C), ln_v_gain[l], ln_v_bias[l],
                   w_spatial[l].astype(bf16), bs_bcast, wa, wb, wc, wo,
                   final_gain, 256, final_norm=(l == depth - 1))
        x2d = x3.reshape(b * s, d)
    return x2d.reshape(b, s, d)
```

```python
import functools
import math

import jax
import jax.numpy as jnp
import numpy as np
from jax import lax
from jax.experimental import pallas as pl
from jax.experimental.pallas import tpu as pltpu

D_MODEL = 2048
MEM_LEN = 256
EPS = 1e-6
HEAD_DIM_A = 128
WIDTH_A = 1536
N_Q_HEADS_A = 12
N_KV_HEADS_A = 4
REP_A = N_Q_HEADS_A // N_KV_HEADS_A
KV_WIDTH_A = 512
BLOCK = 128
WIDTH_B = 1536
CHUNK = 128
N_GROUPS_B = 12
N_HEADS_C = 4
WIDTH_C = 1024
HEAD_DIM_C = 256
NEG_INF = -1e30

SQRT_HALF = math.sqrt(0.5)
LOG2_E = math.log2(math.e)

MIB = 1024 * 1024
V7X_VMEM_BYTES = 64 * MIB
VMEM_LIMIT = V7X_VMEM_BYTES - 8 * MIB
TAIL_VMEM_LIMIT = V7X_VMEM_BYTES - 1 * MIB

RMSNORM_ROWS = 1024
INPROJ_ROWS = 2048
INPROJ_CHUNK_ROWS = 512
ATTN_ROWS = 512
TAIL_ROWS = 256
MEM_KV_COLS = 1024
QK_LOG2_SCALE = LOG2_E / math.sqrt(HEAD_DIM_A)

COL_BLOCK = 512
TRIPLE = 3 * COL_BLOCK
PROJ_SRC_BLOCKS = (0, 1, 2,
                   3, 4, 17,
                   5, 6, 7,
                   14, 15, 16,
                   8, 9, 10,
                   11, 12, 13,
                   18, 19, 20
                   ) + tuple(range(21, 33))
PROJ_PATTERNS = (("plain",) * 3, ("silu",) * 3, ("gelu",) * 3, ("plain", "silu", "silu"),
                 ("sigmoid",) * 3, ("qscale",) * 3)
PROJ_STEP_PATTERN = (5, 0, 1, 1, 2, 2, 3, 4, 4, 4, 4)
N_PROJ = len(PROJ_SRC_BLOCKS) * COL_BLOCK
INPROJ_CHUNKS_PER_ITER = 2
ATTN_LOOKAHEAD = 1
OUT_Q_A_T, OUT_Z_A_T, OUT_Z_B_T, OUT_U_B_T, OUT_V_B_T = 0, 2, 3, 4, 5
OUT_GATES_T = (7, 8, 9, 10)
OUT_K_A, OUT_V_A = 3, 4
OUT_Q_C = (5, 18)
OUT_Z_C = (19, 20)


def _alibi_slopes(n):
    def pow2_slopes(m):
        start = 2.0 ** (-8.0 / m)
        return [start ** (i + 1) for i in range(m)]
    if math.log2(n).is_integer():
        s = pow2_slopes(n)
    else:
        c = 2 ** int(math.floor(math.log2(n)))
        s = pow2_slopes(c) + pow2_slopes(2 * c)[0::2][: n - c]
    return [float(v) for v in np.asarray(s, dtype=np.float32)]


ALIBI_SLOPES = _alibi_slopes(N_Q_HEADS_A)


def _sigmoid(x):
    return 0.5 * (1.0 + jnp.tanh(0.5 * x))


def _activation(x, kind):
    if kind == "plain":
        return x
    if kind == "qscale":
        return x * QK_LOG2_SCALE
    if kind == "silu":
        return x * _sigmoid(x)
    if kind == "gelu":
        return 0.5 * x * (1.0 + lax.erf(x * SQRT_HALF))
    if kind == "sigmoid":
        return _sigmoid(x)
    raise ValueError(kind)


def _aligned(index, multiple):
    return index if isinstance(index, int) else pl.multiple_of(index, multiple)


def _dot(a, b):
    return jnp.dot(a, b, preferred_element_type=jnp.float32)


def _dot_nt(a, b):
    return lax.dot_general(a, b, (((1,), (1,)), ((), ())), preferred_element_type=jnp.float32)


def _rmsnorm_kernel(x_ref, g_ref, o_ref):
    x = x_ref[...]
    ms = jnp.mean(x * x, axis=-1, keepdims=True)
    o_ref[...] = ((x * lax.rsqrt(ms + EPS)) * g_ref[...]).astype(o_ref.dtype)


def _rmsnorm_bf16(x2d, gain, tm):
    t, d = x2d.shape
    return pl.pallas_call(
        _rmsnorm_kernel,
        grid=(t // tm,),
        in_specs=[pl.BlockSpec((tm, d), lambda i: (i, 0)),
                  pl.BlockSpec((1, d), lambda i: (0, 0))],
        out_specs=pl.BlockSpec((tm, d), lambda i: (i, 0)),
        out_shape=jax.ShapeDtypeStruct((t, d), jnp.bfloat16),
        compiler_params=pltpu.CompilerParams(dimension_semantics=("arbitrary",),
                                             vmem_limit_bytes=VMEM_LIMIT),
        name="rmsnorm_bf16",
    )(x2d, gain.reshape(1, d))


def _inproj_kernel(src_ref, pat_ref, h_ref, w0_ref, w1_ref, w2_ref, o_ref, wbuf, *, patterns, chunk):
    jp = pl.program_id(0)
    i = pl.program_id(1)
    tm = h_ref.shape[0]
    slice_rows = w0_ref.shape[0]
    nxt = jp % 2
    cur = (jp + 1) % 2

    n_iter = tm // (INPROJ_CHUNKS_PER_ITER * chunk)
    part_rows = slice_rows // n_iter

    def cast_next_weights(part):
        src_rows = pl.ds(_aligned(part * part_rows, part_rows), part_rows)
        dst_rows = pl.ds(pl.multiple_of(i * slice_rows + part * part_rows, part_rows), part_rows)
        for s, w_ref in enumerate((w0_ref, w1_ref, w2_ref)):
            wbuf[nxt, dst_rows, s * COL_BLOCK:(s + 1) * COL_BLOCK] = (
                w_ref[src_rows, :].astype(wbuf.dtype))

    @pl.when(jp == 0)
    def _():
        for part in range(n_iter):
            cast_next_weights(part)

    pat = pat_ref[jnp.maximum(jp - 1, 0)]
    for p, pattern in enumerate(patterns):
        @pl.when((jp > 0) & (pat == p))
        def _(pattern=pattern):
            def body(it, carry):
                cast_next_weights(it)
                for c in range(INPROJ_CHUNKS_PER_ITER):
                    start = _aligned((it * INPROJ_CHUNKS_PER_ITER + c) * chunk, chunk)
                    rows = pl.ds(start, chunk)
                    hc = h_ref[rows, :]
                    for s in range(3):
                        acc = _dot(hc, wbuf[cur, :, s * COL_BLOCK:(s + 1) * COL_BLOCK])
                        o_ref[rows, s * COL_BLOCK:(s + 1) * COL_BLOCK] = _activation(
                            acc, pattern[s]).astype(o_ref.dtype)
                return carry

            if n_iter == 1:
                body(0, 0)
            else:
                lax.fori_loop(0, n_iter, body, 0)


def _inproj(h, w_in, src_blocks, step_pattern, patterns, tm, chunk):
    t, d = h.shape
    n_steps = len(step_pattern)
    m_tiles = t // tm
    assert len(src_blocks) == 3 * n_steps and d % m_tiles == 0
    slice_rows = d // m_tiles

    def w_spec(s):
        def imap(jp, i, src_ref, pat_ref):
            return (i, src_ref[3 * jnp.minimum(jp, n_steps - 1) + s])
        return pl.BlockSpec((slice_rows, COL_BLOCK), imap)

    def h_map(jp, i, src_ref, pat_ref):
        return (jnp.where(jp == 0, 0, i), 0)

    def o_map(jp, i, src_ref, pat_ref):
        return (jnp.where(jp == 0, 0, i), jnp.maximum(jp - 1, 0))

    return pl.pallas_call(
        functools.partial(_inproj_kernel, patterns=patterns, chunk=chunk),
        grid_spec=pltpu.PrefetchScalarGridSpec(
            num_scalar_prefetch=2,
            grid=(n_steps + 1, m_tiles),
            in_specs=[pl.BlockSpec((tm, d), h_map), w_spec(0), w_spec(1), w_spec(2)],
            out_specs=pl.BlockSpec((tm, TRIPLE), o_map),
            scratch_shapes=[pltpu.VMEM((2, d, TRIPLE), jnp.bfloat16)],
        ),
        out_shape=jax.ShapeDtypeStruct((t, n_steps * TRIPLE), jnp.bfloat16),
        compiler_params=pltpu.CompilerParams(dimension_semantics=("arbitrary", "arbitrary"),
                                             vmem_limit_bytes=VMEM_LIMIT),
        name="inproj",
    )(jnp.asarray(src_blocks, jnp.int32), jnp.asarray(step_pattern, jnp.int32),
      h, w_in, w_in, w_in)


def _attn_kernel(sink_ref, q_ref, kp_ref, kc_ref, kn_ref, vp_ref, vc_ref, vn_ref, z_ref, *rest,
                 n_cast):
    w_refs = rest[:n_cast]
    o_ref = rest[n_cast]
    wo_refs = rest[n_cast + 1:2 * n_cast + 1]
    kbuf, vbuf, bias_ref = rest[2 * n_cast + 1:]
    tq = q_ref.shape[0]
    nblk = tq // BLOCK
    bi = pl.program_id(0)
    n = pl.program_id(1)
    nt = pl.num_programs(1)

    if n_cast:
        step = bi * nt + n
        steps_per_weight = (pl.num_programs(0) * nt) // n_cast
        for k in range(n_cast):
            @pl.when((step >= k * steps_per_weight) & (step < (k + 1) * steps_per_weight))
            def _(k=k):
                wo_refs[k][...] = w_refs[k][...].astype(wo_refs[k].dtype)

    @pl.when((bi == 0) & (n == 0))
    def _():
        row = lax.broadcasted_iota(jnp.int32, (BLOCK, 3 * BLOCK), 0)
        col = lax.broadcasted_iota(jnp.int32, (BLOCK, 3 * BLOCK), 1)
        dist = jnp.abs(row - (col - BLOCK))
        valid = dist <= BLOCK
        distf = dist.astype(jnp.float32)
        for h in range(N_Q_HEADS_A):
            hk, r = divmod(h, REP_A)
            bias_ref[hk, r * BLOCK:(r + 1) * BLOCK, :] = jnp.where(
                valid, (-ALIBI_SLOPES[h] * LOG2_E) * distf, NEG_INF)
        for hk in range(N_KV_HEADS_A):
            vbuf[:, (2 * hk + 1) * HEAD_DIM_A:(2 * hk + 2) * HEAD_DIM_A] = jnp.ones(
                (tq + 2 * BLOCK, HEAD_DIM_A), vbuf.dtype)

    kbuf[0:BLOCK, :] = kp_ref[...]
    kbuf[BLOCK:BLOCK + tq, :] = kc_ref[...]
    kbuf[BLOCK + tq:, :] = kn_ref[...]
    for hk in range(N_KV_HEADS_A):
        src = slice(hk * HEAD_DIM_A, (hk + 1) * HEAD_DIM_A)
        dst = slice(2 * hk * HEAD_DIM_A, (2 * hk + 1) * HEAD_DIM_A)
        vbuf[0:BLOCK, dst] = vp_ref[:, src]
        vbuf[BLOCK:BLOCK + tq, dst] = vc_ref[:, src]
        vbuf[BLOCK + tq:, dst] = vn_ref[:, src]

    edge_p = jnp.where(n == 0, NEG_INF, 0.0)
    edge_n = jnp.where(n == nt - 1, NEG_INF, 0.0)

    units = [(blk, hk) for blk in range(nblk) for hk in range(N_KV_HEADS_A)]

    def scores(unit):
        blk, hk = unit
        rows = slice(blk * BLOCK, (blk + 1) * BLOCK)
        q = jnp.concatenate(
            [q_ref[rows, (hk * REP_A + r) * HEAD_DIM_A:(hk * REP_A + r + 1) * HEAD_DIM_A]
             for r in range(REP_A)], axis=0)
        k = kbuf[blk * BLOCK:(blk + 3) * BLOCK, hk * HEAD_DIM_A:(hk + 1) * HEAD_DIM_A]
        return _dot_nt(q, k)

    pending = [scores(u) for u in units[:ATTN_LOOKAHEAD]]
    for i, (blk, hk) in enumerate(units):
        s = pending.pop(0)
        if i + ATTN_LOOKAHEAD < len(units):
            pending.append(scores(units[i + ATTN_LOOKAHEAD]))
        rows = slice(blk * BLOCK, (blk + 1) * BLOCK)
        krows = slice(blk * BLOCK, (blk + 3) * BLOCK)
        t = bias_ref[hk] + s
        t_p, t_c, t_n = t[:, 0:BLOCK], t[:, BLOCK:2 * BLOCK], t[:, 2 * BLOCK:]
        if blk == 0:
            t_p = t_p + edge_p
        if blk == nblk - 1:
            t_n = t_n + edge_n
        sink2 = jnp.concatenate(
            [jnp.full((BLOCK, BLOCK), sink_ref[hk * REP_A + r] * LOG2_E, jnp.float32)
             for r in range(REP_A)], axis=0)
        m = jnp.max(jnp.maximum(jnp.maximum(t_p, t_c), t_n), axis=-1, keepdims=True)
        m = jnp.maximum(jnp.broadcast_to(m, (REP_A * BLOCK, BLOCK)), sink2)
        p = jnp.concatenate([jnp.exp2(t_p - m), jnp.exp2(t_c - m), jnp.exp2(t_n - m)],
                            axis=1).astype(jnp.bfloat16)
        ov = _dot(p, vbuf[krows, 2 * hk * HEAD_DIM_A:(2 * hk + 2) * HEAD_DIM_A])
        denom = ov[:, HEAD_DIM_A:] + jnp.exp2(sink2 - m)
        o = ov[:, :HEAD_DIM_A] * (1.0 / denom)
        for r in range(REP_A):
            hsl = slice((hk * REP_A + r) * HEAD_DIM_A, (hk * REP_A + r + 1) * HEAD_DIM_A)
            z = z_ref[rows, hsl].astype(jnp.float32)
            o_ref[rows, hsl] = (o[r * BLOCK:(r + 1) * BLOCK] * z).astype(o_ref.dtype)


def _branch_a(proj3, sink, tq, cast_weights):
    b, s, _ = proj3.shape
    nb = s // BLOCK
    per = tq // BLOCK
    nt = s // tq
    n_cast = len(cast_weights)
    steps_per_weight = (b * nt) // n_cast
    assert steps_per_weight * n_cast == b * nt

    def cast_spec(k, w):
        rows, cols = w.shape
        assert rows % steps_per_weight == 0
        def imap(bi, n, sink_ref):
            return (jnp.clip(bi * nt + n - k * steps_per_weight, 0, steps_per_weight - 1), 0)
        return pl.BlockSpec((rows // steps_per_weight, cols), imap)

    cast_specs = [cast_spec(k, w) for k, w in enumerate(cast_weights)]

    def tile(width, idx):
        return pl.BlockSpec((None, tq, width), lambda bi, n, sink_ref: (bi, n, idx))

    def halo(idx, after):
        def imap(bi, n, sink_ref):
            blk = (n + 1) * per if after else n * per - 1
            return (bi, jnp.clip(blk, 0, nb - 1), idx)
        return pl.BlockSpec((None, BLOCK, KV_WIDTH_A), imap)

    in_specs = [
        tile(WIDTH_A, OUT_Q_A_T),
        halo(OUT_K_A, False), tile(KV_WIDTH_A, OUT_K_A), halo(OUT_K_A, True),
        halo(OUT_V_A, False), tile(KV_WIDTH_A, OUT_V_A), halo(OUT_V_A, True),
        tile(WIDTH_A, OUT_Z_A_T),
    ] + cast_specs
    outs = pl.pallas_call(
        functools.partial(_attn_kernel, n_cast=n_cast),
        grid_spec=pltpu.PrefetchScalarGridSpec(
            num_scalar_prefetch=1,
            grid=(b, nt),
            in_specs=in_specs,
            out_specs=[pl.BlockSpec((None, tq, WIDTH_A), lambda bi, n, sink_ref: (bi, n, 0))]
            + cast_specs,
            scratch_shapes=[pltpu.VMEM((tq + 2 * BLOCK, KV_WIDTH_A), jnp.bfloat16),
                            pltpu.VMEM((tq + 2 * BLOCK, 2 * KV_WIDTH_A), jnp.bfloat16),
                            pltpu.VMEM((N_KV_HEADS_A, REP_A * BLOCK, 3 * BLOCK), jnp.float32)],
        ),
        out_shape=[jax.ShapeDtypeStruct((b, s, WIDTH_A), jnp.bfloat16)]
        + [jax.ShapeDtypeStruct(w.shape, jnp.bfloat16) for w in cast_weights],
        compiler_params=pltpu.CompilerParams(dimension_semantics=("arbitrary", "arbitrary"),
                                             vmem_limit_bytes=VMEM_LIMIT),
        name="branch_a_attn",
    )(sink, *([proj3] * 8), *cast_weights)
    return outs[0], outs[1:]


def _gmlp_spatial(v_ref, lng_ref, lnb_ref, ws_ref):
    nc = v_ref.shape[0] // CHUNK
    gv = v_ref[...].astype(jnp.float32)
    mu = jnp.mean(gv, axis=-1, keepdims=True)
    xc = gv - mu
    var = jnp.mean(xc * xc, axis=-1, keepdims=True)
    vn = (xc * lax.rsqrt(var + EPS) * lng_ref[...] + lnb_ref[...]).astype(jnp.bfloat16)
    out = []
    for g in range(N_GROUPS_B):
        gsl = slice(g * CHUNK, (g + 1) * CHUNK)
        rhs = jnp.concatenate([vn[c * CHUNK:(c + 1) * CHUNK, gsl] for c in range(nc)], axis=1)
        out.append(_dot(ws_ref[g], rhs))
    return out


def _gmlp_gate(spatial, u_ref, z_ref, bs_ref, o_ref):
    nc = o_ref.shape[0] // CHUNK
    for g, sg in enumerate(spatial):
        gsl = slice(g * CHUNK, (g + 1) * CHUNK)
        bias = bs_ref[g]
        for c in range(nc):
            rows = slice(c * CHUNK, (c + 1) * CHUNK)
            s_c = sg[:, c * CHUNK:(c + 1) * CHUNK] + bias
            u = u_ref[rows, gsl].astype(jnp.float32)
            z = z_ref[rows, gsl].astype(jnp.float32)
            o_ref[rows, gsl] = (u * s_c * z).astype(o_ref.dtype)


def _mem_kv_kernel(mem_ref, g_ref, w_ref, o_ref, memh_s):
    @pl.when(pl.program_id(0) == 0)
    def _():
        x = mem_ref[...]
        ms = jnp.mean(x * x, axis=-1, keepdims=True)
        memh_s[...] = ((x * lax.rsqrt(ms + EPS)) * g_ref[...]).astype(memh_s.dtype)

    o_ref[...] = _dot(memh_s[...], w_ref[...].astype(jnp.bfloat16)).astype(o_ref.dtype)


def _mem_kv(mem2d, gain, w_kv, tn):
    m, d = mem2d.shape
    n = w_kv.shape[1]
    return pl.pallas_call(
        _mem_kv_kernel,
        grid=(n // tn,),
        in_specs=[pl.BlockSpec((m, d), lambda j: (0, 0), pipeline_mode=pl.Buffered(1)),
                  pl.BlockSpec((1, d), lambda j: (0, 0)),
                  pl.BlockSpec((d, tn), lambda j: (0, j))],
        out_specs=pl.BlockSpec((m, tn), lambda j: (0, j)),
        out_shape=jax.ShapeDtypeStruct((m, n), jnp.bfloat16),
        scratch_shapes=[pltpu.VMEM((m, d), jnp.bfloat16)],
        compiler_params=pltpu.CompilerParams(dimension_semantics=("arbitrary",),
                                             vmem_limit_bytes=VMEM_LIMIT),
        name="mem_kv",
    )(mem2d, gain.reshape(1, d), w_kv)


def _xattn_scores(q_refs, kv_ref):
    out = []
    for h in range(N_HEADS_C):
        cb, off = divmod(h * HEAD_DIM_C, COL_BLOCK)
        q = q_refs[cb][:, off:off + HEAD_DIM_C]
        out.append(_dot_nt(q, kv_ref[:, h * HEAD_DIM_C:(h + 1) * HEAD_DIM_C]))
    return out


def _xattn_values(scores, kv_ref):
    scale = 1.0 / math.sqrt(HEAD_DIM_C)
    out = []
    for h, s in enumerate(scores):
        s = s * scale
        m = jnp.max(s, axis=-1, keepdims=True)
        p = jnp.exp(s - m)
        denom = jnp.sum(p, axis=-1, keepdims=True)
        v = kv_ref[:, WIDTH_C + h * HEAD_DIM_C:WIDTH_C + (h + 1) * HEAD_DIM_C]
        out.append((_dot(p.astype(jnp.bfloat16), v), denom))
    return out


def _xattn_finish(values, z_refs, o_ref):
    for h, (o, denom) in enumerate(values):
        cb, off = divmod(h * HEAD_DIM_C, COL_BLOCK)
        z = z_refs[cb][:, off:off + HEAD_DIM_C].astype(jnp.float32)
        o_ref[:, h * HEAD_DIM_C:(h + 1) * HEAD_DIM_C] = ((o / denom) * z).astype(o_ref.dtype)


def _tail_kernel(x_ref, ga_ref, g0_ref, g1_ref, g2_ref, g3_ref,
                 u_ref, v_ref, zb_ref, qc0_ref, qc1_ref, zc0_ref, zc1_ref, kv_ref,
                 lng_ref, lnb_ref, ws_ref, bs_ref, wa_ref, wb_ref, wc_ref, wo_ref, fg_ref,
                 o_ref, gb_s, gc_s, *, final_norm):
    d = D_MODEL
    g_refs = (g0_ref, g1_ref, g2_ref, g3_ref)

    def gate(branch):
        parts = []
        for cb in range(branch * D_MODEL // COL_BLOCK, (branch + 1) * D_MODEL // COL_BLOCK):
            blk, off = divmod(cb * COL_BLOCK, TRIPLE)
            parts.append(g_refs[blk][:, off:off + COL_BLOCK])
        return jnp.concatenate(parts, axis=1).astype(jnp.float32)

    half = d // 2
    ga = ga_ref[...]
    xs = _xattn_scores((qc0_ref, qc1_ref), kv_ref)
    a_lo = _dot(ga, wa_ref[:, :half])
    spatial = _gmlp_spatial(v_ref, lng_ref, lnb_ref, ws_ref)
    a_hi = _dot(ga, wa_ref[:, half:])
    xv = _xattn_values(xs, kv_ref)
    merged = jnp.concatenate([a_lo, a_hi], axis=1) * gate(0)
    _gmlp_gate(spatial, u_ref, zb_ref, bs_ref, gb_s)
    merged = merged + _dot(gb_s[...], wb_ref[...]) * gate(1)
    _xattn_finish(xv, (zc0_ref, zc1_ref), gc_s)
    merged = merged + _dot(gc_s[...], wc_ref[...]) * gate(2)
    y = x_ref[...] + _dot(merged.astype(jnp.bfloat16), wo_ref[...])
    if final_norm:
        ms = jnp.mean(y * y, axis=-1, keepdims=True)
        y = (y * lax.rsqrt(ms + EPS)) * fg_ref[...]
    o_ref[...] = y


def _tail(x3, ga3, proj3, kv3, ln_gain, ln_bias, w_spatial_bf16, b_spatial_bcast,
          wa, wb, wc, wo, final_gain, tm, final_norm):
    b, s, d = x3.shape

    def rows(width, idx=0):
        return pl.BlockSpec((None, tm, width), lambda bi, i: (bi, i, idx))

    def resident(shape):
        zeros = (0,) * len(shape)
        return pl.BlockSpec(shape, lambda bi, i: zeros, pipeline_mode=pl.Buffered(1))

    in_specs = [
        rows(d), rows(WIDTH_A),
        rows(TRIPLE, OUT_GATES_T[0]), rows(TRIPLE, OUT_GATES_T[1]),
        rows(TRIPLE, OUT_GATES_T[2]), rows(TRIPLE, OUT_GATES_T[3]),
        rows(WIDTH_B, OUT_U_B_T), rows(WIDTH_B, OUT_V_B_T), rows(WIDTH_B, OUT_Z_B_T),
        rows(COL_BLOCK, OUT_Q_C[0]), rows(COL_BLOCK, OUT_Q_C[1]),
        rows(COL_BLOCK, OUT_Z_C[0]), rows(COL_BLOCK, OUT_Z_C[1]),
        pl.BlockSpec((None, MEM_LEN, 2 * WIDTH_C), lambda bi, i: (bi, 0, 0)),
        resident((1, WIDTH_B)), resident((1, WIDTH_B)),
        resident((N_GROUPS_B, CHUNK, CHUNK)), resident((N_GROUPS_B, CHUNK, CHUNK)),
        resident((WIDTH_A, d)), resident((WIDTH_B, d)), resident((WIDTH_C, d)), resident((d, d)),
        resident((1, d)),
    ]
    return pl.pallas_call(
        functools.partial(_tail_kernel, final_norm=final_norm),
        grid=(b, s // tm),
        in_specs=in_specs,
        out_specs=rows(d),
        out_shape=jax.ShapeDtypeStruct((b, s, d), jnp.float32),
        scratch_shapes=[pltpu.VMEM((tm, WIDTH_B), jnp.bfloat16),
                        pltpu.VMEM((tm, WIDTH_C), jnp.bfloat16)],
        compiler_params=pltpu.CompilerParams(dimension_semantics=("arbitrary", "arbitrary"),
                                             vmem_limit_bytes=TAIL_VMEM_LIMIT),
        name="tail_merge_out",
    )(x3, ga3, *([proj3] * 4), *([proj3] * 7), kv3,
      ln_gain.reshape(1, WIDTH_B), ln_bias.reshape(1, WIDTH_B), w_spatial_bf16, b_spatial_bcast,
      wa, wb, wc, wo, final_gain.reshape(1, d))


def kernel(x, mem, norm_gain, mem_norm_gain, w_in, sink, ln_v_gain, ln_v_bias, w_spatial, b_spatial,
           w_kv_mem, w_br_a, w_br_b, w_br_c, w_out, final_gain):
    b, s, d = x.shape
    depth = w_in.shape[0]
    bf16 = jnp.bfloat16
    x2d = x.reshape(b * s, d)
    mem2d = mem.reshape(b * MEM_LEN, d)
    for l in range(depth):
        h = _rmsnorm_bf16(x2d, norm_gain[l], RMSNORM_ROWS)
        proj = _inproj(h, w_in[l], PROJ_SRC_BLOCKS, PROJ_STEP_PATTERN, PROJ_PATTERNS,
                       INPROJ_ROWS, INPROJ_CHUNK_ROWS)
        proj3 = proj.reshape(b, s, N_PROJ)

        ga3, (wa, wb, wc, wo) = _branch_a(proj3, sink[l].astype(jnp.float32), ATTN_ROWS,
                                          (w_br_a[l], w_br_b[l], w_br_c[l], w_out[l]))

        kv = _mem_kv(mem2d, mem_norm_gain[l], w_kv_mem[l], MEM_KV_COLS)

        bs_bcast = jnp.broadcast_to(b_spatial[l].astype(jnp.float32)[:, :, None],
                                    (N_GROUPS_B, CHUNK, CHUNK))
        x3 = _tail(x2d.reshape(b, s, d), ga3, proj3,
                   kv.reshape(b, MEM_LEN, 2 * WIDTH_C), ln_v_gain[l], ln_v_bias[l],
                   w_spatial[l].astype(bf16), bs_bcast, wa, wb, wc, wo,
                   final_gain, TAIL_ROWS, final_norm=(l == depth - 1))
        x2d = x3.reshape(b * s, d)
    return x2d.reshape(b, s, d)
```

```python
import functools
import math

import jax
import jax.numpy as jnp
import numpy as np
from jax import lax
from jax.experimental import pallas as pl
from jax.experimental.pallas import tpu as pltpu

D_MODEL = 2048
MEM_LEN = 256
EPS = 1e-6
HEAD_DIM_A = 128
WIDTH_A = 1536
N_Q_HEADS_A = 12
N_KV_HEADS_A = 4
REP_A = N_Q_HEADS_A // N_KV_HEADS_A
KV_WIDTH_A = 512
BLOCK = 128
WIDTH_B = 1536
CHUNK = 128
N_GROUPS_B = 12
N_HEADS_C = 4
WIDTH_C = 1024
HEAD_DIM_C = 256
NEG_INF = -1e30

SQRT_HALF = math.sqrt(0.5)
LOG2_E = math.log2(math.e)

MIB = 1024 * 1024
V7X_VMEM_BYTES = 64 * MIB
VMEM_LIMIT = V7X_VMEM_BYTES - 8 * MIB
TAIL_VMEM_LIMIT = V7X_VMEM_BYTES - 1 * MIB

RMSNORM_ROWS = 1024
INPROJ_ROWS = 2048
INPROJ_CHUNK_ROWS = 512
ATTN_ROWS = 512
TAIL_ROWS = 256
QK_LOG2_SCALE = LOG2_E / math.sqrt(HEAD_DIM_A)

COL_BLOCK = 512
TRIPLE = 3 * COL_BLOCK
PROJ_SRC_BLOCKS = (0, 1, 2,
                   3, 4, 17,
                   5, 6, 7,
                   14, 15, 16,
                   8, 9, 10,
                   11, 12, 13,
                   18, 19, 20
                   ) + tuple(range(21, 33))
PROJ_PATTERNS = (("plain",) * 3, ("silu",) * 3, ("gelu",) * 3, ("plain", "silu", "silu"),
                 ("sigmoid",) * 3, ("qscale",) * 3)
PROJ_STEP_PATTERN = (5, 0, 1, 1, 2, 2, 3, 4, 4, 4, 4)
N_PROJ = len(PROJ_SRC_BLOCKS) * COL_BLOCK
INPROJ_CHUNKS_PER_ITER = 2
INPROJ_UNROLLED_PATTERNS = (4,)
ATTN_LOOKAHEAD = 1
OUT_Q_A_T, OUT_Z_A_T, OUT_Z_B_T, OUT_U_B_T, OUT_V_B_T = 0, 2, 3, 4, 5
OUT_GATES_T = (7, 8, 9, 10)
OUT_K_A, OUT_V_A = 3, 4
OUT_Q_C = (5, 18)
OUT_Z_C = (19, 20)


def _alibi_slopes(n):
    def pow2_slopes(m):
        start = 2.0 ** (-8.0 / m)
        return [start ** (i + 1) for i in range(m)]
    if math.log2(n).is_integer():
        s = pow2_slopes(n)
    else:
        c = 2 ** int(math.floor(math.log2(n)))
        s = pow2_slopes(c) + pow2_slopes(2 * c)[0::2][: n - c]
    return [float(v) for v in np.asarray(s, dtype=np.float32)]


ALIBI_SLOPES = _alibi_slopes(N_Q_HEADS_A)


def _sigmoid(x):
    return 0.5 * (1.0 + jnp.tanh(0.5 * x))


def _activation(x, kind):
    if kind == "plain":
        return x
    if kind == "qscale":
        return x * QK_LOG2_SCALE
    if kind == "silu":
        return x * _sigmoid(x)
    if kind == "gelu":
        return 0.5 * x * (1.0 + lax.erf(x * SQRT_HALF))
    if kind == "sigmoid":
        return _sigmoid(x)
    raise ValueError(kind)


def _aligned(index, multiple):
    return index if isinstance(index, int) else pl.multiple_of(index, multiple)


def _dot(a, b):
    return jnp.dot(a, b, preferred_element_type=jnp.float32)


def _dot_nt(a, b):
    return lax.dot_general(a, b, (((1,), (1,)), ((), ())), preferred_element_type=jnp.float32)


def _inproj_kernel(src_ref, pat_ref, h_ref, w0_ref, w1_ref, w2_ref, o_ref, wbuf, *, patterns, chunk):
    jp = pl.program_id(0)
    i = pl.program_id(1)
    tm = h_ref.shape[0]
    slice_rows = w0_ref.shape[0]
    nxt = jp % 2
    cur = (jp + 1) % 2

    def cast_next_weights(part, n_parts):
        part_rows = slice_rows // n_parts
        src_rows = pl.ds(_aligned(part * part_rows, part_rows), part_rows)
        dst_rows = pl.ds(pl.multiple_of(i * slice_rows + part * part_rows, part_rows), part_rows)
        for s, w_ref in enumerate((w0_ref, w1_ref, w2_ref)):
            wbuf[nxt, dst_rows, s * COL_BLOCK:(s + 1) * COL_BLOCK] = (
                w_ref[src_rows, :].astype(wbuf.dtype))

    @pl.when(jp == 0)
    def _():
        cast_next_weights(0, 1)

    pat = pat_ref[jnp.maximum(jp - 1, 0)]
    for p, pattern in enumerate(patterns):
        @pl.when((jp > 0) & (pat == p))
        def _(p=p, pattern=pattern):
            chunks_per_iter = tm // chunk if p in INPROJ_UNROLLED_PATTERNS else INPROJ_CHUNKS_PER_ITER
            n_iter = tm // (chunks_per_iter * chunk)

            def body(it, carry):
                cast_next_weights(it, n_iter)
                for c in range(chunks_per_iter):
                    start = _aligned((it * chunks_per_iter + c) * chunk, chunk)
                    rows = pl.ds(start, chunk)
                    hc = h_ref[rows, :]
                    for s in range(3):
                        acc = _dot(hc, wbuf[cur, :, s * COL_BLOCK:(s + 1) * COL_BLOCK])
                        o_ref[rows, s * COL_BLOCK:(s + 1) * COL_BLOCK] = _activation(
                            acc, pattern[s]).astype(o_ref.dtype)
                return carry

            if n_iter == 1:
                body(0, 0)
            else:
                lax.fori_loop(0, n_iter, body, 0)


def _inproj(h, w_in, src_blocks, step_pattern, patterns, tm, chunk):
    t, d = h.shape
    n_steps = len(step_pattern)
    m_tiles = t // tm
    assert len(src_blocks) == 3 * n_steps and d % m_tiles == 0
    slice_rows = d // m_tiles

    def w_spec(s):
        def imap(jp, i, src_ref, pat_ref):
            return (i, src_ref[3 * jnp.minimum(jp, n_steps - 1) + s])
        return pl.BlockSpec((slice_rows, COL_BLOCK), imap)

    def h_map(jp, i, src_ref, pat_ref):
        return (jnp.where(jp == 0, 0, i), 0)

    def o_map(jp, i, src_ref, pat_ref):
        return (jnp.where(jp == 0, 0, i), jnp.maximum(jp - 1, 0))

    return pl.pallas_call(
        functools.partial(_inproj_kernel, patterns=patterns, chunk=chunk),
        grid_spec=pltpu.PrefetchScalarGridSpec(
            num_scalar_prefetch=2,
            grid=(n_steps + 1, m_tiles),
            in_specs=[pl.BlockSpec((tm, d), h_map), w_spec(0), w_spec(1), w_spec(2)],
            out_specs=pl.BlockSpec((tm, TRIPLE), o_map),
            scratch_shapes=[pltpu.VMEM((2, d, TRIPLE), jnp.bfloat16)],
        ),
        out_shape=jax.ShapeDtypeStruct((t, n_steps * TRIPLE), jnp.bfloat16),
        compiler_params=pltpu.CompilerParams(dimension_semantics=("arbitrary", "arbitrary"),
                                             vmem_limit_bytes=VMEM_LIMIT),
        name="inproj",
    )(jnp.asarray(src_blocks, jnp.int32), jnp.asarray(step_pattern, jnp.int32),
      h, w_in, w_in, w_in)


def _attn_kernel(sink_ref, q_ref, kp_ref, kc_ref, kn_ref, vp_ref, vc_ref, vn_ref, z_ref, *rest,
                 n_cast):
    w_refs = rest[:n_cast]
    o_ref = rest[n_cast]
    wo_refs = rest[n_cast + 1:2 * n_cast + 1]
    kbuf, vbuf, bias_ref = rest[2 * n_cast + 1:]
    tq = q_ref.shape[0]
    nblk = tq // BLOCK
    bi = pl.program_id(0)
    n = pl.program_id(1)
    nt = pl.num_programs(1)

    if n_cast:
        step = bi * nt + n
        steps_per_weight = (pl.num_programs(0) * nt) // n_cast
        for k in range(n_cast):
            @pl.when((step >= k * steps_per_weight) & (step < (k + 1) * steps_per_weight))
            def _(k=k):
                wo_refs[k][...] = w_refs[k][...].astype(wo_refs[k].dtype)

    @pl.when((bi == 0) & (n == 0))
    def _():
        row = lax.broadcasted_iota(jnp.int32, (BLOCK, 3 * BLOCK), 0)
        col = lax.broadcasted_iota(jnp.int32, (BLOCK, 3 * BLOCK), 1)
        dist = jnp.abs(row - (col - BLOCK))
        valid = dist <= BLOCK
        distf = dist.astype(jnp.float32)
        for h in range(N_Q_HEADS_A):
            hk, r = divmod(h, REP_A)
            bias_ref[hk, r * BLOCK:(r + 1) * BLOCK, :] = jnp.where(
                valid, (-ALIBI_SLOPES[h] * LOG2_E) * distf, NEG_INF)
        for hk in range(N_KV_HEADS_A):
            vbuf[:, (2 * hk + 1) * HEAD_DIM_A:(2 * hk + 2) * HEAD_DIM_A] = jnp.ones(
                (tq + 2 * BLOCK, HEAD_DIM_A), vbuf.dtype)

    kbuf[0:BLOCK, :] = kp_ref[...]
    kbuf[BLOCK:BLOCK + tq, :] = kc_ref[...]
    kbuf[BLOCK + tq:, :] = kn_ref[...]
    for hk in range(N_KV_HEADS_A):
        src = slice(hk * HEAD_DIM_A, (hk + 1) * HEAD_DIM_A)
        dst = slice(2 * hk * HEAD_DIM_A, (2 * hk + 1) * HEAD_DIM_A)
        vbuf[0:BLOCK, dst] = vp_ref[:, src]
        vbuf[BLOCK:BLOCK + tq, dst] = vc_ref[:, src]
        vbuf[BLOCK + tq:, dst] = vn_ref[:, src]

    edge_p = jnp.where(n == 0, NEG_INF, 0.0)
    edge_n = jnp.where(n == nt - 1, NEG_INF, 0.0)

    units = [(blk, hk) for blk in range(nblk) for hk in range(N_KV_HEADS_A)]

    def scores(unit):
        blk, hk = unit
        rows = slice(blk * BLOCK, (blk + 1) * BLOCK)
        q = jnp.concatenate(
            [q_ref[rows, (hk * REP_A + r) * HEAD_DIM_A:(hk * REP_A + r + 1) * HEAD_DIM_A]
             for r in range(REP_A)], axis=0)
        k = kbuf[blk * BLOCK:(blk + 3) * BLOCK, hk * HEAD_DIM_A:(hk + 1) * HEAD_DIM_A]
        return _dot_nt(q, k)

    pending = [scores(u) for u in units[:ATTN_LOOKAHEAD]]
    for i, (blk, hk) in enumerate(units):
        s = pending.pop(0)
        if i + ATTN_LOOKAHEAD < len(units):
            pending.append(scores(units[i + ATTN_LOOKAHEAD]))
        rows = slice(blk * BLOCK, (blk + 1) * BLOCK)
        krows = slice(blk * BLOCK, (blk + 3) * BLOCK)
        t = bias_ref[hk] + s
        t_p, t_c, t_n = t[:, 0:BLOCK], t[:, BLOCK:2 * BLOCK], t[:, 2 * BLOCK:]
        if blk == 0:
            t_p = t_p + edge_p
        if blk == nblk - 1:
            t_n = t_n + edge_n
        sink2 = jnp.concatenate(
            [jnp.full((BLOCK, BLOCK), sink_ref[hk * REP_A + r] * LOG2_E, jnp.float32)
             for r in range(REP_A)], axis=0)
        m = jnp.max(jnp.maximum(jnp.maximum(t_p, t_c), t_n), axis=-1, keepdims=True)
        m = jnp.maximum(jnp.broadcast_to(m, (REP_A * BLOCK, BLOCK)), sink2)
        p = jnp.concatenate([jnp.exp2(t_p - m), jnp.exp2(t_c - m), jnp.exp2(t_n - m)],
                            axis=1).astype(jnp.bfloat16)
        ov = _dot(p, vbuf[krows, 2 * hk * HEAD_DIM_A:(2 * hk + 2) * HEAD_DIM_A])
        denom = ov[:, HEAD_DIM_A:] + jnp.exp2(sink2 - m)
        o = ov[:, :HEAD_DIM_A] * (1.0 / denom)
        for r in range(REP_A):
            hsl = slice((hk * REP_A + r) * HEAD_DIM_A, (hk * REP_A + r + 1) * HEAD_DIM_A)
            z = z_ref[rows, hsl].astype(jnp.float32)
            o_ref[rows, hsl] = (o[r * BLOCK:(r + 1) * BLOCK] * z).astype(o_ref.dtype)


def _branch_a(proj3, sink, tq, cast_weights):
    b, s, _ = proj3.shape
    nb = s // BLOCK
    per = tq // BLOCK
    nt = s // tq
    n_cast = len(cast_weights)
    steps_per_weight = (b * nt) // n_cast
    assert steps_per_weight * n_cast == b * nt

    def cast_spec(k, w):
        rows, cols = w.shape
        assert rows % steps_per_weight == 0
        def imap(bi, n, sink_ref):
            return (jnp.clip(bi * nt + n - k * steps_per_weight, 0, steps_per_weight - 1), 0)
        return pl.BlockSpec((rows // steps_per_weight, cols), imap)

    cast_specs = [cast_spec(k, w) for k, w in enumerate(cast_weights)]

    def tile(width, idx):
        return pl.BlockSpec((None, tq, width), lambda bi, n, sink_ref: (bi, n, idx))

    def halo(idx, after):
        def imap(bi, n, sink_ref):
            blk = (n + 1) * per if after else n * per - 1
            return (bi, jnp.clip(blk, 0, nb - 1), idx)
        return pl.BlockSpec((None, BLOCK, KV_WIDTH_A), imap)

    in_specs = [
        tile(WIDTH_A, OUT_Q_A_T),
        halo(OUT_K_A, False), tile(KV_WIDTH_A, OUT_K_A), halo(OUT_K_A, True),
        halo(OUT_V_A, False), tile(KV_WIDTH_A, OUT_V_A), halo(OUT_V_A, True),
        tile(WIDTH_A, OUT_Z_A_T),
    ] + cast_specs
    outs = pl.pallas_call(
        functools.partial(_attn_kernel, n_cast=n_cast),
        grid_spec=pltpu.PrefetchScalarGridSpec(
            num_scalar_prefetch=1,
            grid=(b, nt),
            in_specs=in_specs,
            out_specs=[pl.BlockSpec((None, tq, WIDTH_A), lambda bi, n, sink_ref: (bi, n, 0))]
            + cast_specs,
            scratch_shapes=[pltpu.VMEM((tq + 2 * BLOCK, KV_WIDTH_A), jnp.bfloat16),
                            pltpu.VMEM((tq + 2 * BLOCK, 2 * KV_WIDTH_A), jnp.bfloat16),
                            pltpu.VMEM((N_KV_HEADS_A, REP_A * BLOCK, 3 * BLOCK), jnp.float32)],
        ),
        out_shape=[jax.ShapeDtypeStruct((b, s, WIDTH_A), jnp.bfloat16)]
        + [jax.ShapeDtypeStruct(w.shape, jnp.bfloat16) for w in cast_weights],
        compiler_params=pltpu.CompilerParams(dimension_semantics=("arbitrary", "arbitrary"),
                                             vmem_limit_bytes=VMEM_LIMIT),
        name="branch_a_attn",
    )(sink, *([proj3] * 8), *cast_weights)
    return outs[0], outs[1:]


def _gmlp_spatial(v_ref, lng_ref, lnb_ref, ws_ref):
    nc = v_ref.shape[0] // CHUNK
    gv = v_ref[...].astype(jnp.float32)
    mu = jnp.mean(gv, axis=-1, keepdims=True)
    xc = gv - mu
    var = jnp.mean(xc * xc, axis=-1, keepdims=True)
    vn = (xc * lax.rsqrt(var + EPS) * lng_ref[...] + lnb_ref[...]).astype(jnp.bfloat16)
    out = []
    for g in range(N_GROUPS_B):
        gsl = slice(g * CHUNK, (g + 1) * CHUNK)
        rhs = jnp.concatenate([vn[c * CHUNK:(c + 1) * CHUNK, gsl] for c in range(nc)], axis=1)
        out.append(_dot(ws_ref[g], rhs))
    return out


def _gmlp_gate(spatial, u_ref, z_ref, bs_ref, o_ref):
    nc = o_ref.shape[0] // CHUNK
    for g, sg in enumerate(spatial):
        gsl = slice(g * CHUNK, (g + 1) * CHUNK)
        bias = bs_ref[g]
        for c in range(nc):
            rows = slice(c * CHUNK, (c + 1) * CHUNK)
            s_c = sg[:, c * CHUNK:(c + 1) * CHUNK] + bias
            u = u_ref[rows, gsl].astype(jnp.float32)
            z = z_ref[rows, gsl].astype(jnp.float32)
            o_ref[rows, gsl] = (u * s_c * z).astype(o_ref.dtype)


def _rmsnorm_rows(x, gain):
    ms = jnp.mean(x * x, axis=-1, keepdims=True)
    return (x * lax.rsqrt(ms + EPS)) * gain


def _prep_kernel(x_ref, g_ref, mem_ref, mg_ref, w_ref, h_ref, kv_ref, memh_s):
    @pl.when(pl.program_id(0) == 0)
    def _():
        memh_s[...] = _rmsnorm_rows(mem_ref[...], mg_ref[...]).astype(memh_s.dtype)

    h_ref[...] = _rmsnorm_rows(x_ref[...], g_ref[...]).astype(h_ref.dtype)
    kv_ref[...] = _dot(memh_s[...], w_ref[...].astype(jnp.bfloat16)).astype(kv_ref.dtype)


def _prep(x2d, gain, mem2d, mem_gain, w_kv, tm):
    t, d = x2d.shape
    m = mem2d.shape[0]
    n = w_kv.shape[1]
    steps = t // tm
    tn = n // steps
    assert tn * steps == n and tn % 128 == 0
    return pl.pallas_call(
        _prep_kernel,
        grid=(steps,),
        in_specs=[pl.BlockSpec((tm, d), lambda i: (i, 0)),
                  pl.BlockSpec((1, d), lambda i: (0, 0)),
                  pl.BlockSpec((m, d), lambda i: (0, 0), pipeline_mode=pl.Buffered(1)),
                  pl.BlockSpec((1, d), lambda i: (0, 0)),
                  pl.BlockSpec((d, tn), lambda i: (0, i))],
        out_specs=[pl.BlockSpec((tm, d), lambda i: (i, 0)),
                   pl.BlockSpec((m, tn), lambda i: (0, i))],
        out_shape=[jax.ShapeDtypeStruct((t, d), jnp.bfloat16),
                   jax.ShapeDtypeStruct((m, n), jnp.bfloat16)],
        scratch_shapes=[pltpu.VMEM((m, d), jnp.bfloat16)],
        compiler_params=pltpu.CompilerParams(dimension_semantics=("arbitrary",),
                                             vmem_limit_bytes=VMEM_LIMIT),
        name="prep_rmsnorm_memkv",
    )(x2d, gain.reshape(1, d), mem2d, mem_gain.reshape(1, d), w_kv)


def _xattn_scores(q_refs, kv_ref):
    out = []
    for h in range(N_HEADS_C):
        cb, off = divmod(h * HEAD_DIM_C, COL_BLOCK)
        q = q_refs[cb][:, off:off + HEAD_DIM_C]
        out.append(_dot_nt(q, kv_ref[:, h * HEAD_DIM_C:(h + 1) * HEAD_DIM_C]))
    return out


def _xattn_values(scores, kv_ref):
    scale = 1.0 / math.sqrt(HEAD_DIM_C)
    out = []
    for h, s in enumerate(scores):
        s = s * scale
        m = jnp.max(s, axis=-1, keepdims=True)
        p = jnp.exp(s - m)
        denom = jnp.sum(p, axis=-1, keepdims=True)
        v = kv_ref[:, WIDTH_C + h * HEAD_DIM_C:WIDTH_C + (h + 1) * HEAD_DIM_C]
        out.append((_dot(p.astype(jnp.bfloat16), v), denom))
    return out


def _xattn_finish(values, z_refs, o_ref):
    for h, (o, denom) in enumerate(values):
        cb, off = divmod(h * HEAD_DIM_C, COL_BLOCK)
        z = z_refs[cb][:, off:off + HEAD_DIM_C].astype(jnp.float32)
        o_ref[:, h * HEAD_DIM_C:(h + 1) * HEAD_DIM_C] = ((o / denom) * z).astype(o_ref.dtype)


def _tail_kernel(x_ref, ga_ref, g0_ref, g1_ref, g2_ref, g3_ref,
                 u_ref, v_ref, zb_ref, qc0_ref, qc1_ref, zc0_ref, zc1_ref, kv_ref,
                 lng_ref, lnb_ref, ws_ref, bs_ref, wa_ref, wb_ref, wc_ref, wo_ref, fg_ref,
                 o_ref, gb_s, gc_s, *, final_norm):
    d = D_MODEL
    g_refs = (g0_ref, g1_ref, g2_ref, g3_ref)

    def gate(branch):
        parts = []
        for cb in range(branch * D_MODEL // COL_BLOCK, (branch + 1) * D_MODEL // COL_BLOCK):
            blk, off = divmod(cb * COL_BLOCK, TRIPLE)
            parts.append(g_refs[blk][:, off:off + COL_BLOCK])
        return jnp.concatenate(parts, axis=1).astype(jnp.float32)

    half = d // 2
    ga = ga_ref[...]
    xs = _xattn_scores((qc0_ref, qc1_ref), kv_ref)
    a_lo = _dot(ga, wa_ref[:, :half])
    spatial = _gmlp_spatial(v_ref, lng_ref, lnb_ref, ws_ref)
    a_hi = _dot(ga, wa_ref[:, half:])
    xv = _xattn_values(xs, kv_ref)
    merged = jnp.concatenate([a_lo, a_hi], axis=1) * gate(0)
    _gmlp_gate(spatial, u_ref, zb_ref, bs_ref, gb_s)
    merged = merged + _dot(gb_s[...], wb_ref[...]) * gate(1)
    _xattn_finish(xv, (zc0_ref, zc1_ref), gc_s)
    merged = merged + _dot(gc_s[...], wc_ref[...]) * gate(2)
    y = x_ref[...] + _dot(merged.astype(jnp.bfloat16), wo_ref[...])
    if final_norm:
        ms = jnp.mean(y * y, axis=-1, keepdims=True)
        y = (y * lax.rsqrt(ms + EPS)) * fg_ref[...]
    o_ref[...] = y


def _tail(x3, ga3, proj3, kv3, ln_gain, ln_bias, w_spatial_bf16, b_spatial_bcast,
          wa, wb, wc, wo, final_gain, tm, final_norm):
    b, s, d = x3.shape

    def rows(width, idx=0):
        return pl.BlockSpec((None, tm, width), lambda bi, i: (bi, i, idx))

    def resident(shape):
        zeros = (0,) * len(shape)
        return pl.BlockSpec(shape, lambda bi, i: zeros, pipeline_mode=pl.Buffered(1))

    in_specs = [
        rows(d), rows(WIDTH_A),
        rows(TRIPLE, OUT_GATES_T[0]), rows(TRIPLE, OUT_GATES_T[1]),
        rows(TRIPLE, OUT_GATES_T[2]), rows(TRIPLE, OUT_GATES_T[3]),
        rows(WIDTH_B, OUT_U_B_T), rows(WIDTH_B, OUT_V_B_T), rows(WIDTH_B, OUT_Z_B_T),
        rows(COL_BLOCK, OUT_Q_C[0]), rows(COL_BLOCK, OUT_Q_C[1]),
        rows(COL_BLOCK, OUT_Z_C[0]), rows(COL_BLOCK, OUT_Z_C[1]),
        pl.BlockSpec((None, MEM_LEN, 2 * WIDTH_C), lambda bi, i: (bi, 0, 0)),
        resident((1, WIDTH_B)), resident((1, WIDTH_B)),
        resident((N_GROUPS_B, CHUNK, CHUNK)), resident((N_GROUPS_B, CHUNK, CHUNK)),
        resident((WIDTH_A, d)), resident((WIDTH_B, d)), resident((WIDTH_C, d)), resident((d, d)),
        resident((1, d)),
    ]
    return pl.pallas_call(
        functools.partial(_tail_kernel, final_norm=final_norm),
        grid=(b, s // tm),
        in_specs=in_specs,
        out_specs=rows(d),
        out_shape=jax.ShapeDtypeStruct((b, s, d), jnp.float32),
        scratch_shapes=[pltpu.VMEM((tm, WIDTH_B), jnp.bfloat16),
                        pltpu.VMEM((tm, WIDTH_C), jnp.bfloat16)],
        compiler_params=pltpu.CompilerParams(dimension_semantics=("arbitrary", "arbitrary"),
                                             vmem_limit_bytes=TAIL_VMEM_LIMIT),
        name="tail_merge_out",
    )(x3, ga3, *([proj3] * 4), *([proj3] * 7), kv3,
      ln_gain.reshape(1, WIDTH_B), ln_bias.reshape(1, WIDTH_B), w_spatial_bf16, b_spatial_bcast,
      wa, wb, wc, wo, final_gain.reshape(1, d))


def kernel(x, mem, norm_gain, mem_norm_gain, w_in, sink, ln_v_gain, ln_v_bias, w_spatial, b_spatial,
           w_kv_mem, w_br_a, w_br_b, w_br_c, w_out, final_gain):
    b, s, d = x.shape
    depth = w_in.shape[0]
    bf16 = jnp.bfloat16
    x2d = x.reshape(b * s, d)
    mem2d = mem.reshape(b * MEM_LEN, d)
    for l in range(depth):
        h, kv = _prep(x2d, norm_gain[l], mem2d, mem_norm_gain[l], w_kv_mem[l], RMSNORM_ROWS)
        proj = _inproj(h, w_in[l], PROJ_SRC_BLOCKS, PROJ_STEP_PATTERN, PROJ_PATTERNS,
                       INPROJ_ROWS, INPROJ_CHUNK_ROWS)
        proj3 = proj.reshape(b, s, N_PROJ)

        ga3, (wa, wb, wc, wo) = _branch_a(proj3, sink[l].astype(jnp.float32), ATTN_ROWS,
                                          (w_br_a[l], w_br_b[l], w_br_c[l], w_out[l]))

        bs_bcast = jnp.broadcast_to(b_spatial[l].astype(jnp.float32)[:, :, None],
                                    (N_GROUPS_B, CHUNK, CHUNK))
        x3 = _tail(x2d.reshape(b, s, d), ga3, proj3,
                   kv.reshape(b, MEM_LEN, 2 * WIDTH_C), ln_v_gain[l], ln_v_bias[l],
                   w_spatial[l].astype(bf16), bs_bcast, wa, wb, wc, wo,
                   final_gain, TAIL_ROWS, final_norm=(l == depth - 1))
        x2d = x3.reshape(b * s, d)
    return x2d.reshape(b, s, d)
```

```python
import functools
import math

import jax
import jax.numpy as jnp
import numpy as np
from jax import lax
from jax.experimental import pallas as pl
from jax.experimental.pallas import tpu as pltpu

D_MODEL = 2048
MEM_LEN = 256
EPS = 1e-6
HEAD_DIM_A = 128
WIDTH_A = 1536
N_Q_HEADS_A = 12
N_KV_HEADS_A = 4
REP_A = N_Q_HEADS_A // N_KV_HEADS_A
KV_WIDTH_A = 512
BLOCK = 128
WIDTH_B = 1536
CHUNK = 128
N_GROUPS_B = 12
N_HEADS_C = 4
WIDTH_C = 1024
HEAD_DIM_C = 256
NEG_INF = -1e30

SQRT_HALF = math.sqrt(0.5)
LOG2_E = math.log2(math.e)

MIB = 1024 * 1024
V7X_VMEM_BYTES = 64 * MIB
VMEM_LIMIT = V7X_VMEM_BYTES - 8 * MIB
TAIL_VMEM_LIMIT = V7X_VMEM_BYTES - 1 * MIB

RMSNORM_ROWS = 1024
INPROJ_ROWS = 2048
INPROJ_CHUNK_ROWS = 512
ATTN_ROWS = 512
TAIL_ROWS = 256
MEM_KV_COLS = 512
QK_LOG2_SCALE = LOG2_E / math.sqrt(HEAD_DIM_A)

COL_BLOCK = 512
TRIPLE = 3 * COL_BLOCK
PROJ_SRC_BLOCKS = (0, 1, 2,
                   3, 4, 17,
                   5, 6, 7,
                   14, 15, 16,
                   8, 9, 10,
                   11, 12, 13,
                   18, 19, 20
                   ) + tuple(range(21, 33))
PROJ_PATTERNS = (("plain",) * 3, ("silu",) * 3, ("gelu",) * 3, ("plain", "silu", "silu"),
                 ("sigmoid",) * 3, ("qscale",) * 3)
PROJ_STEP_PATTERN = (5, 0, 1, 1, 2, 2, 3, 4, 4, 4, 4)
N_PROJ = len(PROJ_SRC_BLOCKS) * COL_BLOCK
INPROJ_CHUNKS_PER_ITER = 2
ATTN_LOOKAHEAD = 1
OUT_Q_A_T, OUT_Z_A_T, OUT_Z_B_T, OUT_U_B_T, OUT_V_B_T = 0, 2, 3, 4, 5
OUT_GATES_T = (7, 8, 9, 10)
OUT_K_A, OUT_V_A = 3, 4
OUT_Q_C = (5, 18)
OUT_Z_C = (19, 20)


def _alibi_slopes(n):
    def pow2_slopes(m):
        start = 2.0 ** (-8.0 / m)
        return [start ** (i + 1) for i in range(m)]
    if math.log2(n).is_integer():
        s = pow2_slopes(n)
    else:
        c = 2 ** int(math.floor(math.log2(n)))
        s = pow2_slopes(c) + pow2_slopes(2 * c)[0::2][: n - c]
    return [float(v) for v in np.asarray(s, dtype=np.float32)]


ALIBI_SLOPES = _alibi_slopes(N_Q_HEADS_A)


def _sigmoid(x):
    return 0.5 * (1.0 + jnp.tanh(0.5 * x))


def _activation(x, kind):
    if kind == "plain":
        return x
    if kind == "qscale":
        return x * QK_LOG2_SCALE
    if kind == "silu":
        return x * _sigmoid(x)
    if kind == "gelu":
        return 0.5 * x * (1.0 + lax.erf(x * SQRT_HALF))
    if kind == "sigmoid":
        return _sigmoid(x)
    raise ValueError(kind)


def _aligned(index, multiple):
    return index if isinstance(index, int) else pl.multiple_of(index, multiple)


def _dot(a, b):
    return jnp.dot(a, b, preferred_element_type=jnp.float32)


def _dot_nt(a, b):
    return lax.dot_general(a, b, (((1,), (1,)), ((), ())), preferred_element_type=jnp.float32)


def _rmsnorm_rows(x, gain):
    ms = jnp.mean(x * x, axis=-1, keepdims=True)
    return (x * lax.rsqrt(ms + EPS)) * gain


def _rmsnorm_kernel(x_ref, g_ref, o_ref):
    o_ref[...] = _rmsnorm_rows(x_ref[...], g_ref[...]).astype(o_ref.dtype)


def _rmsnorm_bf16(x2d, gain, tm):
    t, d = x2d.shape
    return pl.pallas_call(
        _rmsnorm_kernel,
        grid=(t // tm,),
        in_specs=[pl.BlockSpec((tm, d), lambda i: (i, 0)),
                  pl.BlockSpec((1, d), lambda i: (0, 0))],
        out_specs=pl.BlockSpec((tm, d), lambda i: (i, 0)),
        out_shape=jax.ShapeDtypeStruct((t, d), jnp.bfloat16),
        compiler_params=pltpu.CompilerParams(dimension_semantics=("arbitrary",),
                                             vmem_limit_bytes=VMEM_LIMIT),
        name="rmsnorm_bf16",
    )(x2d, gain.reshape(1, d))


def _inproj_kernel(src_ref, pat_ref, h_ref, w0_ref, w1_ref, w2_ref, o_ref, wbuf, *, patterns, chunk):
    jp = pl.program_id(0)
    i = pl.program_id(1)
    tm = h_ref.shape[0]
    slice_rows = w0_ref.shape[0]
    nxt = jp % 2
    cur = (jp + 1) % 2

    def cast_next_weights(part, n_parts):
        part_rows = slice_rows // n_parts
        src_rows = pl.ds(_aligned(part * part_rows, part_rows), part_rows)
        dst_rows = pl.ds(pl.multiple_of(i * slice_rows + part * part_rows, part_rows), part_rows)
        for s, w_ref in enumerate((w0_ref, w1_ref, w2_ref)):
            wbuf[nxt, dst_rows, s * COL_BLOCK:(s + 1) * COL_BLOCK] = (
                w_ref[src_rows, :].astype(wbuf.dtype))

    @pl.when(jp == 0)
    def _():
        cast_next_weights(0, 1)

    pat = pat_ref[jnp.maximum(jp - 1, 0)]
    for p, pattern in enumerate(patterns):
        @pl.when((jp > 0) & (pat == p))
        def _(pattern=pattern):
            n_iter = tm // (INPROJ_CHUNKS_PER_ITER * chunk)

            def body(it, carry):
                cast_next_weights(it, n_iter)
                for c in range(INPROJ_CHUNKS_PER_ITER):
                    start = _aligned((it * INPROJ_CHUNKS_PER_ITER + c) * chunk, chunk)
                    rows = pl.ds(start, chunk)
                    hc = h_ref[rows, :]
                    for s in range(3):
                        acc = _dot(hc, wbuf[cur, :, s * COL_BLOCK:(s + 1) * COL_BLOCK])
                        o_ref[rows, s * COL_BLOCK:(s + 1) * COL_BLOCK] = _activation(
                            acc, pattern[s]).astype(o_ref.dtype)
                return carry

            if n_iter == 1:
                body(0, 0)
            else:
                lax.fori_loop(0, n_iter, body, 0)


def _inproj(h, w_in, src_blocks, step_pattern, patterns, tm, chunk):
    t, d = h.shape
    n_steps = len(step_pattern)
    m_tiles = t // tm
    assert len(src_blocks) == 3 * n_steps and d % m_tiles == 0
    slice_rows = d // m_tiles

    def w_spec(s):
        def imap(jp, i, src_ref, pat_ref):
            return (i, src_ref[3 * jnp.minimum(jp, n_steps - 1) + s])
        return pl.BlockSpec((slice_rows, COL_BLOCK), imap)

    def h_map(jp, i, src_ref, pat_ref):
        return (jnp.where(jp == 0, 0, i), 0)

    def o_map(jp, i, src_ref, pat_ref):
        return (jnp.where(jp == 0, 0, i), jnp.maximum(jp - 1, 0))

    return pl.pallas_call(
        functools.partial(_inproj_kernel, patterns=patterns, chunk=chunk),
        grid_spec=pltpu.PrefetchScalarGridSpec(
            num_scalar_prefetch=2,
            grid=(n_steps + 1, m_tiles),
            in_specs=[pl.BlockSpec((tm, d), h_map), w_spec(0), w_spec(1), w_spec(2)],
            out_specs=pl.BlockSpec((tm, TRIPLE), o_map),
            scratch_shapes=[pltpu.VMEM((2, d, TRIPLE), jnp.bfloat16)],
        ),
        out_shape=jax.ShapeDtypeStruct((t, n_steps * TRIPLE), jnp.bfloat16),
        compiler_params=pltpu.CompilerParams(dimension_semantics=("arbitrary", "arbitrary"),
                                             vmem_limit_bytes=VMEM_LIMIT),
        name="inproj",
    )(jnp.asarray(src_blocks, jnp.int32), jnp.asarray(step_pattern, jnp.int32),
      h, w_in, w_in, w_in)


def _attn_kernel(sink_ref, q_ref, kp_ref, kc_ref, kn_ref, vp_ref, vc_ref, vn_ref, z_ref, *rest,
                 n_cast):
    w_refs = rest[:n_cast]
    o_ref = rest[n_cast]
    wo_refs = rest[n_cast + 1:2 * n_cast + 1]
    kbuf, vbuf, bias_ref = rest[2 * n_cast + 1:]
    tq = q_ref.shape[0]
    nblk = tq // BLOCK
    bi = pl.program_id(0)
    n = pl.program_id(1)
    nt = pl.num_programs(1)

    if n_cast:
        step = bi * nt + n
        steps_per_weight = (pl.num_programs(0) * nt) // n_cast
        for k in range(n_cast):
            @pl.when((step >= k * steps_per_weight) & (step < (k + 1) * steps_per_weight))
            def _(k=k):
                wo_refs[k][...] = w_refs[k][...].astype(wo_refs[k].dtype)

    @pl.when((bi == 0) & (n == 0))
    def _():
        row = lax.broadcasted_iota(jnp.int32, (BLOCK, 3 * BLOCK), 0)
        col = lax.broadcasted_iota(jnp.int32, (BLOCK, 3 * BLOCK), 1)
        dist = jnp.abs(row - (col - BLOCK))
        valid = dist <= BLOCK
        distf = dist.astype(jnp.float32)
        for h in range(N_Q_HEADS_A):
            hk, r = divmod(h, REP_A)
            bias_ref[hk, r * BLOCK:(r + 1) * BLOCK, :] = jnp.where(
                valid, (-ALIBI_SLOPES[h] * LOG2_E) * distf, NEG_INF)
        for hk in range(N_KV_HEADS_A):
            vbuf[:, (2 * hk + 1) * HEAD_DIM_A:(2 * hk + 2) * HEAD_DIM_A] = jnp.ones(
                (tq + 2 * BLOCK, HEAD_DIM_A), vbuf.dtype)

    kbuf[0:BLOCK, :] = kp_ref[...]
    kbuf[BLOCK:BLOCK + tq, :] = kc_ref[...]
    kbuf[BLOCK + tq:, :] = kn_ref[...]
    for hk in range(N_KV_HEADS_A):
        src = slice(hk * HEAD_DIM_A, (hk + 1) * HEAD_DIM_A)
        dst = slice(2 * hk * HEAD_DIM_A, (2 * hk + 1) * HEAD_DIM_A)
        vbuf[0:BLOCK, dst] = vp_ref[:, src]
        vbuf[BLOCK:BLOCK + tq, dst] = vc_ref[:, src]
        vbuf[BLOCK + tq:, dst] = vn_ref[:, src]

    edge_p = jnp.where(n == 0, NEG_INF, 0.0)
    edge_n = jnp.where(n == nt - 1, NEG_INF, 0.0)

    units = [(blk, hk) for blk in range(nblk) for hk in range(N_KV_HEADS_A)]

    def scores(unit):
        blk, hk = unit
        rows = slice(blk * BLOCK, (blk + 1) * BLOCK)
        q = jnp.concatenate(
            [q_ref[rows, (hk * REP_A + r) * HEAD_DIM_A:(hk * REP_A + r + 1) * HEAD_DIM_A]
             for r in range(REP_A)], axis=0)
        k = kbuf[blk * BLOCK:(blk + 3) * BLOCK, hk * HEAD_DIM_A:(hk + 1) * HEAD_DIM_A]
        return _dot_nt(q, k)

    pending = [scores(u) for u in units[:ATTN_LOOKAHEAD]]
    for i, (blk, hk) in enumerate(units):
        s = pending.pop(0)
        if i + ATTN_LOOKAHEAD < len(units):
            pending.append(scores(units[i + ATTN_LOOKAHEAD]))
        rows = slice(blk * BLOCK, (blk + 1) * BLOCK)
        krows = slice(blk * BLOCK, (blk + 3) * BLOCK)
        t = bias_ref[hk] + s
        t_p, t_c, t_n = t[:, 0:BLOCK], t[:, BLOCK:2 * BLOCK], t[:, 2 * BLOCK:]
        if blk == 0:
            t_p = t_p + edge_p
        if blk == nblk - 1:
            t_n = t_n + edge_n
        sink2 = jnp.concatenate(
            [jnp.full((BLOCK, BLOCK), sink_ref[hk * REP_A + r] * LOG2_E, jnp.float32)
             for r in range(REP_A)], axis=0)
        m = jnp.max(jnp.maximum(jnp.maximum(t_p, t_c), t_n), axis=-1, keepdims=True)
        m = jnp.maximum(jnp.broadcast_to(m, (REP_A * BLOCK, BLOCK)), sink2)
        p = jnp.concatenate([jnp.exp2(t_p - m), jnp.exp2(t_c - m), jnp.exp2(t_n - m)],
                            axis=1).astype(jnp.bfloat16)
        ov = _dot(p, vbuf[krows, 2 * hk * HEAD_DIM_A:(2 * hk + 2) * HEAD_DIM_A])
        denom = ov[:, HEAD_DIM_A:] + jnp.exp2(sink2 - m)
        o = ov[:, :HEAD_DIM_A] * (1.0 / denom)
        for r in range(REP_A):
            hsl = slice((hk * REP_A + r) * HEAD_DIM_A, (hk * REP_A + r + 1) * HEAD_DIM_A)
            z = z_ref[rows, hsl].astype(jnp.float32)
            o_ref[rows, hsl] = (o[r * BLOCK:(r + 1) * BLOCK] * z).astype(o_ref.dtype)


def _branch_a(proj3, sink, tq, cast_weights):
    b, s, _ = proj3.shape
    nb = s // BLOCK
    per = tq // BLOCK
    nt = s // tq
    n_cast = len(cast_weights)
    steps_per_weight = (b * nt) // n_cast
    assert steps_per_weight * n_cast == b * nt

    def cast_spec(k, w):
        rows, cols = w.shape
        assert rows % steps_per_weight == 0
        def imap(bi, n, sink_ref):
            return (jnp.clip(bi * nt + n - k * steps_per_weight, 0, steps_per_weight - 1), 0)
        return pl.BlockSpec((rows // steps_per_weight, cols), imap)

    cast_specs = [cast_spec(k, w) for k, w in enumerate(cast_weights)]

    def tile(width, idx):
        return pl.BlockSpec((None, tq, width), lambda bi, n, sink_ref: (bi, n, idx))

    def halo(idx, after):
        def imap(bi, n, sink_ref):
            blk = (n + 1) * per if after else n * per - 1
            return (bi, jnp.clip(blk, 0, nb - 1), idx)
        return pl.BlockSpec((None, BLOCK, KV_WIDTH_A), imap)

    in_specs = [
        tile(WIDTH_A, OUT_Q_A_T),
        halo(OUT_K_A, False), tile(KV_WIDTH_A, OUT_K_A), halo(OUT_K_A, True),
        halo(OUT_V_A, False), tile(KV_WIDTH_A, OUT_V_A), halo(OUT_V_A, True),
        tile(WIDTH_A, OUT_Z_A_T),
    ] + cast_specs
    outs = pl.pallas_call(
        functools.partial(_attn_kernel, n_cast=n_cast),
        grid_spec=pltpu.PrefetchScalarGridSpec(
            num_scalar_prefetch=1,
            grid=(b, nt),
            in_specs=in_specs,
            out_specs=[pl.BlockSpec((None, tq, WIDTH_A), lambda bi, n, sink_ref: (bi, n, 0))]
            + cast_specs,
            scratch_shapes=[pltpu.VMEM((tq + 2 * BLOCK, KV_WIDTH_A), jnp.bfloat16),
                            pltpu.VMEM((tq + 2 * BLOCK, 2 * KV_WIDTH_A), jnp.bfloat16),
                            pltpu.VMEM((N_KV_HEADS_A, REP_A * BLOCK, 3 * BLOCK), jnp.float32)],
        ),
        out_shape=[jax.ShapeDtypeStruct((b, s, WIDTH_A), jnp.bfloat16)]
        + [jax.ShapeDtypeStruct(w.shape, jnp.bfloat16) for w in cast_weights],
        compiler_params=pltpu.CompilerParams(dimension_semantics=("arbitrary", "arbitrary"),
                                             vmem_limit_bytes=VMEM_LIMIT),
        name="branch_a_attn",
    )(sink, *([proj3] * 8), *cast_weights)
    return outs[0], outs[1:]


def _gmlp_spatial(v_ref, lng_ref, lnb_ref, ws_ref):
    nc = v_ref.shape[0] // CHUNK
    gv = v_ref[...].astype(jnp.float32)
    mu = jnp.mean(gv, axis=-1, keepdims=True)
    xc = gv - mu
    var = jnp.mean(xc * xc, axis=-1, keepdims=True)
    vn = (xc * lax.rsqrt(var + EPS) * lng_ref[...] + lnb_ref[...]).astype(jnp.bfloat16)
    out = []
    for g in range(N_GROUPS_B):
        gsl = slice(g * CHUNK, (g + 1) * CHUNK)
        rhs = jnp.concatenate([vn[c * CHUNK:(c + 1) * CHUNK, gsl] for c in range(nc)], axis=1)
        out.append(_dot(ws_ref[g], rhs))
    return out


def _gmlp_gate(spatial, u_ref, z_ref, bs_ref, o_ref):
    nc = o_ref.shape[0] // CHUNK
    for g, sg in enumerate(spatial):
        gsl = slice(g * CHUNK, (g + 1) * CHUNK)
        bias = bs_ref[g]
        for c in range(nc):
            rows = slice(c * CHUNK, (c + 1) * CHUNK)
            s_c = sg[:, c * CHUNK:(c + 1) * CHUNK] + bias
            u = u_ref[rows, gsl].astype(jnp.float32)
            z = z_ref[rows, gsl].astype(jnp.float32)
            o_ref[rows, gsl] = (u * s_c * z).astype(o_ref.dtype)


def _mem_kv_kernel(mem_ref, g_ref, w_ref, o_ref, memh_s):
    @pl.when(pl.program_id(0) == 0)
    def _():
        memh_s[...] = _rmsnorm_rows(mem_ref[...], g_ref[...]).astype(memh_s.dtype)

    o_ref[...] = _dot(memh_s[...], w_ref[...].astype(jnp.bfloat16)).astype(o_ref.dtype)


def _mem_kv(mem2d, gain, w_kv, tn):
    m, d = mem2d.shape
    n = w_kv.shape[1]
    return pl.pallas_call(
        _mem_kv_kernel,
        grid=(n // tn,),
        in_specs=[pl.BlockSpec((m, d), lambda j: (0, 0), pipeline_mode=pl.Buffered(1)),
                  pl.BlockSpec((1, d), lambda j: (0, 0)),
                  pl.BlockSpec((d, tn), lambda j: (0, j))],
        out_specs=pl.BlockSpec((m, tn), lambda j: (0, j)),
        out_shape=jax.ShapeDtypeStruct((m, n), jnp.bfloat16),
        scratch_shapes=[pltpu.VMEM((m, d), jnp.bfloat16)],
        compiler_params=pltpu.CompilerParams(dimension_semantics=("arbitrary",),
                                             vmem_limit_bytes=VMEM_LIMIT),
        name="mem_kv",
    )(mem2d, gain.reshape(1, d), w_kv)


def _xattn_scores(q_refs, kv_ref):
    out = []
    for h in range(N_HEADS_C):
        cb, off = divmod(h * HEAD_DIM_C, COL_BLOCK)
        q = q_refs[cb][:, off:off + HEAD_DIM_C]
        out.append(_dot_nt(q, kv_ref[:, h * HEAD_DIM_C:(h + 1) * HEAD_DIM_C]))
    return out


def _xattn_values(scores, kv_ref):
    scale = 1.0 / math.sqrt(HEAD_DIM_C)
    out = []
    for h, s in enumerate(scores):
        s = s * scale
        m = jnp.max(s, axis=-1, keepdims=True)
        p = jnp.exp(s - m)
        denom = jnp.sum(p, axis=-1, keepdims=True)
        v = kv_ref[:, WIDTH_C + h * HEAD_DIM_C:WIDTH_C + (h + 1) * HEAD_DIM_C]
        out.append((_dot(p.astype(jnp.bfloat16), v), denom))
    return out


def _xattn_finish(values, z_refs, o_ref):
    for h, (o, denom) in enumerate(values):
        cb, off = divmod(h * HEAD_DIM_C, COL_BLOCK)
        z = z_refs[cb][:, off:off + HEAD_DIM_C].astype(jnp.float32)
        o_ref[:, h * HEAD_DIM_C:(h + 1) * HEAD_DIM_C] = ((o / denom) * z).astype(o_ref.dtype)


def _tail_kernel(x_ref, ga_ref, g0_ref, g1_ref, g2_ref, g3_ref,
                 u_ref, v_ref, zb_ref, qc0_ref, qc1_ref, zc0_ref, zc1_ref, kv_ref,
                 lng_ref, lnb_ref, ws_ref, bs_ref, wa_ref, wb_ref, wc_ref, wo_ref, fg_ref,
                 o_ref, gb_s, gc_s, *, final_norm):
    d = D_MODEL
    g_refs = (g0_ref, g1_ref, g2_ref, g3_ref)

    def gate(branch):
        parts = []
        for cb in range(branch * D_MODEL // COL_BLOCK, (branch + 1) * D_MODEL // COL_BLOCK):
            blk, off = divmod(cb * COL_BLOCK, TRIPLE)
            parts.append(g_refs[blk][:, off:off + COL_BLOCK])
        return jnp.concatenate(parts, axis=1).astype(jnp.float32)

    half = d // 2
    ga = ga_ref[...]
    xs = _xattn_scores((qc0_ref, qc1_ref), kv_ref)
    a_lo = _dot(ga, wa_ref[:, :half])
    spatial = _gmlp_spatial(v_ref, lng_ref, lnb_ref, ws_ref)
    a_hi = _dot(ga, wa_ref[:, half:])
    xv = _xattn_values(xs, kv_ref)
    merged = jnp.concatenate([a_lo, a_hi], axis=1) * gate(0)
    _gmlp_gate(spatial, u_ref, zb_ref, bs_ref, gb_s)
    merged = merged + _dot(gb_s[...], wb_ref[...]) * gate(1)
    _xattn_finish(xv, (zc0_ref, zc1_ref), gc_s)
    merged = merged + _dot(gc_s[...], wc_ref[...]) * gate(2)
    y = x_ref[...] + _dot(merged.astype(jnp.bfloat16), wo_ref[...])
    if final_norm:
        ms = jnp.mean(y * y, axis=-1, keepdims=True)
        y = (y * lax.rsqrt(ms + EPS)) * fg_ref[...]
    o_ref[...] = y


def _tail(x3, ga3, proj3, kv3, ln_gain, ln_bias, w_spatial_bf16, b_spatial_bcast,
          wa, wb, wc, wo, final_gain, tm, final_norm):
    b, s, d = x3.shape

    def rows(width, idx=0):
        return pl.BlockSpec((None, tm, width), lambda bi, i: (bi, i, idx))

    def resident(shape):
        zeros = (0,) * len(shape)
        return pl.BlockSpec(shape, lambda bi, i: zeros, pipeline_mode=pl.Buffered(1))

    in_specs = [
        rows(d), rows(WIDTH_A),
        rows(TRIPLE, OUT_GATES_T[0]), rows(TRIPLE, OUT_GATES_T[1]),
        rows(TRIPLE, OUT_GATES_T[2]), rows(TRIPLE, OUT_GATES_T[3]),
        rows(WIDTH_B, OUT_U_B_T), rows(WIDTH_B, OUT_V_B_T), rows(WIDTH_B, OUT_Z_B_T),
        rows(COL_BLOCK, OUT_Q_C[0]), rows(COL_BLOCK, OUT_Q_C[1]),
        rows(COL_BLOCK, OUT_Z_C[0]), rows(COL_BLOCK, OUT_Z_C[1]),
        pl.BlockSpec((None, MEM_LEN, 2 * WIDTH_C), lambda bi, i: (bi, 0, 0)),
        resident((1, WIDTH_B)), resident((1, WIDTH_B)),
        resident((N_GROUPS_B, CHUNK, CHUNK)), resident((N_GROUPS_B, CHUNK, CHUNK)),
        resident((WIDTH_A, d)), resident((WIDTH_B, d)), resident((WIDTH_C, d)), resident((d, d)),
        resident((1, d)),
    ]
    return pl.pallas_call(
        functools.partial(_tail_kernel, final_norm=final_norm),
        grid=(b, s // tm),
        in_specs=in_specs,
        out_specs=rows(d),
        out_shape=jax.ShapeDtypeStruct((b, s, d), jnp.float32),
        scratch_shapes=[pltpu.VMEM((tm, WIDTH_B), jnp.bfloat16),
                        pltpu.VMEM((tm, WIDTH_C), jnp.bfloat16)],
        compiler_params=pltpu.CompilerParams(dimension_semantics=("arbitrary", "arbitrary"),
                                             vmem_limit_bytes=TAIL_VMEM_LIMIT),
        name="tail_merge_out",
    )(x3, ga3, *([proj3] * 4), *([proj3] * 7), kv3,
      ln_gain.reshape(1, WIDTH_B), ln_bias.reshape(1, WIDTH_B), w_spatial_bf16, b_spatial_bcast,
      wa, wb, wc, wo, final_gain.reshape(1, d))


def kernel(x, mem, norm_gain, mem_norm_gain, w_in, sink, ln_v_gain, ln_v_bias, w_spatial, b_spatial,
           w_kv_mem, w_br_a, w_br_b, w_br_c, w_out, final_gain):
    b, s, d = x.shape
    depth = w_in.shape[0]
    bf16 = jnp.bfloat16
    x2d = x.reshape(b * s, d)
    mem2d = mem.reshape(b * MEM_LEN, d)
    for l in range(depth):
        h = _rmsnorm_bf16(x2d, norm_gain[l], RMSNORM_ROWS)
        kv = _mem_kv(mem2d, mem_norm_gain[l], w_kv_mem[l], MEM_KV_COLS)
        proj = _inproj(h, w_in[l], PROJ_SRC_BLOCKS, PROJ_STEP_PATTERN, PROJ_PATTERNS,
                       INPROJ_ROWS, INPROJ_CHUNK_ROWS)
        proj3 = proj.reshape(b, s, N_PROJ)

        ga3, (wa, wb, wc, wo) = _branch_a(proj3, sink[l].astype(jnp.float32), ATTN_ROWS,
                                          (w_br_a[l], w_br_b[l], w_br_c[l], w_out[l]))

        bs_bcast = jnp.broadcast_to(b_spatial[l].astype(jnp.float32)[:, :, None],
                                    (N_GROUPS_B, CHUNK, CHUNK))
        x3 = _tail(x2d.reshape(b, s, d), ga3, proj3,
                   kv.reshape(b, MEM_LEN, 2 * WIDTH_C), ln_v_gain[l], ln_v_bias[l],
                   w_spatial[l].astype(bf16), bs_bcast, wa, wb, wc, wo,
                   final_gain, TAIL_ROWS, final_norm=(l == depth - 1))
        x2d = x3.reshape(b * s, d)
    return x2d.reshape(b, s, d)
```

```python
import functools
import math

import jax
import jax.numpy as jnp
import numpy as np
from jax import lax
from jax.experimental import pallas as pl
from jax.experimental.pallas import tpu as pltpu

D_MODEL = 2048
MEM_LEN = 256
EPS = 1e-6
HEAD_DIM_A = 128
WIDTH_A = 1536
N_Q_HEADS_A = 12
N_KV_HEADS_A = 4
REP_A = N_Q_HEADS_A // N_KV_HEADS_A
KV_WIDTH_A = 512
BLOCK = 128
WIDTH_B = 1536
CHUNK = 128
N_GROUPS_B = 12
N_HEADS_C = 4
WIDTH_C = 1024
HEAD_DIM_C = 256
NEG_INF = -1e30

SQRT_HALF = math.sqrt(0.5)
LOG2_E = math.log2(math.e)

MIB = 1024 * 1024
V7X_VMEM_BYTES = 64 * MIB
VMEM_LIMIT = V7X_VMEM_BYTES - 8 * MIB
TAIL_VMEM_LIMIT = V7X_VMEM_BYTES - 1 * MIB

QPROJ_ROWS = 1024
QPROJ_CHUNK_ROWS = 256
INPROJ_ROWS = 2048
INPROJ_CHUNK_ROWS = 512
ATTN_ROWS = 512
TAIL_ROWS = 256
MEM_KV_COLS = 512
QK_LOG2_SCALE = LOG2_E / math.sqrt(HEAD_DIM_A)

COL_BLOCK = 512
TRIPLE = 3 * COL_BLOCK
PROJ_SRC_BLOCKS = (3, 4, 17,
                   5, 6, 7,
                   14, 15, 16,
                   8, 9, 10,
                   11, 12, 13,
                   18, 19, 20
                   ) + tuple(range(21, 33))
PROJ_PATTERNS = (("plain",) * 3, ("silu",) * 3, ("gelu",) * 3, ("plain", "silu", "silu"),
                 ("sigmoid",) * 3)
PROJ_STEP_PATTERN = (0, 1, 1, 2, 2, 3, 4, 4, 4, 4)
N_PROJ = len(PROJ_SRC_BLOCKS) * COL_BLOCK
INPROJ_CHUNKS_PER_ITER = 2
ATTN_LOOKAHEAD = 1
OUT_Z_A_T, OUT_Z_B_T, OUT_U_B_T, OUT_V_B_T = 1, 2, 3, 4
OUT_GATES_T = (6, 7, 8, 9)
OUT_K_A, OUT_V_A = 0, 1
OUT_Q_C = (2, 15)
OUT_Z_C = (16, 17)


def _alibi_slopes(n):
    def pow2_slopes(m):
        start = 2.0 ** (-8.0 / m)
        return [start ** (i + 1) for i in range(m)]
    if math.log2(n).is_integer():
        s = pow2_slopes(n)
    else:
        c = 2 ** int(math.floor(math.log2(n)))
        s = pow2_slopes(c) + pow2_slopes(2 * c)[0::2][: n - c]
    return [float(v) for v in np.asarray(s, dtype=np.float32)]


ALIBI_SLOPES = _alibi_slopes(N_Q_HEADS_A)


def _sigmoid(x):
    return 0.5 * (1.0 + jnp.tanh(0.5 * x))


def _activation(x, kind):
    if kind == "plain":
        return x
    if kind == "silu":
        return x * _sigmoid(x)
    if kind == "gelu":
        return 0.5 * x * (1.0 + lax.erf(x * SQRT_HALF))
    if kind == "sigmoid":
        return _sigmoid(x)
    raise ValueError(kind)


def _aligned(index, multiple):
    return index if isinstance(index, int) else pl.multiple_of(index, multiple)


def _dot(a, b):
    return jnp.dot(a, b, preferred_element_type=jnp.float32)


def _dot_nt(a, b):
    return lax.dot_general(a, b, (((1,), (1,)), ((), ())), preferred_element_type=jnp.float32)


def _rmsnorm_rows(x, gain):
    ms = jnp.mean(x * x, axis=-1, keepdims=True)
    return (x * lax.rsqrt(ms + EPS)) * gain


def _qproj_kernel(x_ref, g_ref, w_ref, h_ref, q_ref, wbf_s, *, chunk):
    @pl.when(pl.program_id(0) == 0)
    def _():
        wbf_s[...] = w_ref[...].astype(wbf_s.dtype)

    for c in range(x_ref.shape[0] // chunk):
        rows = slice(c * chunk, (c + 1) * chunk)
        h = _rmsnorm_rows(x_ref[rows, :], g_ref[...]).astype(h_ref.dtype)
        h_ref[rows, :] = h
        q_ref[rows, :] = (_dot(h, wbf_s[...]) * QK_LOG2_SCALE).astype(q_ref.dtype)


def _qproj(x2d, gain, w_in, tm, chunk):
    t, d = x2d.shape
    return pl.pallas_call(
        functools.partial(_qproj_kernel, chunk=chunk),
        grid=(t // tm,),
        in_specs=[pl.BlockSpec((tm, d), lambda i: (i, 0)),
                  pl.BlockSpec((1, d), lambda i: (0, 0)),
                  pl.BlockSpec((d, WIDTH_A), lambda i: (0, 0), pipeline_mode=pl.Buffered(1))],
        out_specs=[pl.BlockSpec((tm, d), lambda i: (i, 0)),
                   pl.BlockSpec((tm, WIDTH_A), lambda i: (i, 0))],
        out_shape=[jax.ShapeDtypeStruct((t, d), jnp.bfloat16),
                   jax.ShapeDtypeStruct((t, WIDTH_A), jnp.bfloat16)],
        scratch_shapes=[pltpu.VMEM((d, WIDTH_A), jnp.bfloat16)],
        compiler_params=pltpu.CompilerParams(dimension_semantics=("arbitrary",),
                                             vmem_limit_bytes=VMEM_LIMIT),
        name="rmsnorm_qproj",
    )(x2d, gain.reshape(1, d), w_in)


def _inproj_kernel(src_ref, pat_ref, h_ref, w0_ref, w1_ref, w2_ref, o_ref, wbuf, *, patterns, chunk):
    jp = pl.program_id(0)
    i = pl.program_id(1)
    tm = h_ref.shape[0]
    slice_rows = w0_ref.shape[0]
    nxt = jp % 2
    cur = (jp + 1) % 2

    def cast_next_weights(part, n_parts):
        part_rows = slice_rows // n_parts
        src_rows = pl.ds(_aligned(part * part_rows, part_rows), part_rows)
        dst_rows = pl.ds(pl.multiple_of(i * slice_rows + part * part_rows, part_rows), part_rows)
        for s, w_ref in enumerate((w0_ref, w1_ref, w2_ref)):
            wbuf[nxt, dst_rows, s * COL_BLOCK:(s + 1) * COL_BLOCK] = (
                w_ref[src_rows, :].astype(wbuf.dtype))

    @pl.when(jp == 0)
    def _():
        cast_next_weights(0, 1)

    pat = pat_ref[jnp.maximum(jp - 1, 0)]
    for p, pattern in enumerate(patterns):
        @pl.when((jp > 0) & (pat == p))
        def _(pattern=pattern):
            n_iter = tm // (INPROJ_CHUNKS_PER_ITER * chunk)

            def body(it, carry):
                cast_next_weights(it, n_iter)
                for c in range(INPROJ_CHUNKS_PER_ITER):
                    start = _aligned((it * INPROJ_CHUNKS_PER_ITER + c) * chunk, chunk)
                    rows = pl.ds(start, chunk)
                    hc = h_ref[rows, :]
                    for s in range(3):
                        acc = _dot(hc, wbuf[cur, :, s * COL_BLOCK:(s + 1) * COL_BLOCK])
                        o_ref[rows, s * COL_BLOCK:(s + 1) * COL_BLOCK] = _activation(
                            acc, pattern[s]).astype(o_ref.dtype)
                return carry

            if n_iter == 1:
                body(0, 0)
            else:
                lax.fori_loop(0, n_iter, body, 0)


def _inproj(h, w_in, src_blocks, step_pattern, patterns, tm, chunk):
    t, d = h.shape
    n_steps = len(step_pattern)
    m_tiles = t // tm
    assert len(src_blocks) == 3 * n_steps and d % m_tiles == 0
    slice_rows = d // m_tiles

    def w_spec(s):
        def imap(jp, i, src_ref, pat_ref):
            return (i, src_ref[3 * jnp.minimum(jp, n_steps - 1) + s])
        return pl.BlockSpec((slice_rows, COL_BLOCK), imap)

    def h_map(jp, i, src_ref, pat_ref):
        return (jnp.where(jp == 0, 0, i), 0)

    def o_map(jp, i, src_ref, pat_ref):
        return (jnp.where(jp == 0, 0, i), jnp.maximum(jp - 1, 0))

    return pl.pallas_call(
        functools.partial(_inproj_kernel, patterns=patterns, chunk=chunk),
        grid_spec=pltpu.PrefetchScalarGridSpec(
            num_scalar_prefetch=2,
            grid=(n_steps + 1, m_tiles),
            in_specs=[pl.BlockSpec((tm, d), h_map), w_spec(0), w_spec(1), w_spec(2)],
            out_specs=pl.BlockSpec((tm, TRIPLE), o_map),
            scratch_shapes=[pltpu.VMEM((2, d, TRIPLE), jnp.bfloat16)],
        ),
        out_shape=jax.ShapeDtypeStruct((t, n_steps * TRIPLE), jnp.bfloat16),
        compiler_params=pltpu.CompilerParams(dimension_semantics=("arbitrary", "arbitrary"),
                                             vmem_limit_bytes=VMEM_LIMIT),
        name="inproj",
    )(jnp.asarray(src_blocks, jnp.int32), jnp.asarray(step_pattern, jnp.int32),
      h, w_in, w_in, w_in)


def _attn_kernel(sink_ref, q_ref, kp_ref, kc_ref, kn_ref, vp_ref, vc_ref, vn_ref, z_ref, *rest,
                 n_cast):
    w_refs = rest[:n_cast]
    o_ref = rest[n_cast]
    wo_refs = rest[n_cast + 1:2 * n_cast + 1]
    kbuf, vbuf, bias_ref = rest[2 * n_cast + 1:]
    tq = q_ref.shape[0]
    nblk = tq // BLOCK
    bi = pl.program_id(0)
    n = pl.program_id(1)
    nt = pl.num_programs(1)

    if n_cast:
        step = bi * nt + n
        steps_per_weight = (pl.num_programs(0) * nt) // n_cast
        for k in range(n_cast):
            @pl.when((step >= k * steps_per_weight) & (step < (k + 1) * steps_per_weight))
            def _(k=k):
                wo_refs[k][...] = w_refs[k][...].astype(wo_refs[k].dtype)

    @pl.when((bi == 0) & (n == 0))
    def _():
        row = lax.broadcasted_iota(jnp.int32, (BLOCK, 3 * BLOCK), 0)
        col = lax.broadcasted_iota(jnp.int32, (BLOCK, 3 * BLOCK), 1)
        dist = jnp.abs(row - (col - BLOCK))
        valid = dist <= BLOCK
        distf = dist.astype(jnp.float32)
        for h in range(N_Q_HEADS_A):
            hk, r = divmod(h, REP_A)
            bias_ref[hk, r * BLOCK:(r + 1) * BLOCK, :] = jnp.where(
                valid, (-ALIBI_SLOPES[h] * LOG2_E) * distf, NEG_INF)
        for hk in range(N_KV_HEADS_A):
            vbuf[:, (2 * hk + 1) * HEAD_DIM_A:(2 * hk + 2) * HEAD_DIM_A] = jnp.ones(
                (tq + 2 * BLOCK, HEAD_DIM_A), vbuf.dtype)

    kbuf[0:BLOCK, :] = kp_ref[...]
    kbuf[BLOCK:BLOCK + tq, :] = kc_ref[...]
    kbuf[BLOCK + tq:, :] = kn_ref[...]
    for hk in range(N_KV_HEADS_A):
        src = slice(hk * HEAD_DIM_A, (hk + 1) * HEAD_DIM_A)
        dst = slice(2 * hk * HEAD_DIM_A, (2 * hk + 1) * HEAD_DIM_A)
        vbuf[0:BLOCK, dst] = vp_ref[:, src]
        vbuf[BLOCK:BLOCK + tq, dst] = vc_ref[:, src]
        vbuf[BLOCK + tq:, dst] = vn_ref[:, src]

    edge_p = jnp.where(n == 0, NEG_INF, 0.0)
    edge_n = jnp.where(n == nt - 1, NEG_INF, 0.0)

    units = [(blk, hk) for blk in range(nblk) for hk in range(N_KV_HEADS_A)]

    def scores(unit):
        blk, hk = unit
        rows = slice(blk * BLOCK, (blk + 1) * BLOCK)
        q = jnp.concatenate(
            [q_ref[rows, (hk * REP_A + r) * HEAD_DIM_A:(hk * REP_A + r + 1) * HEAD_DIM_A]
             for r in range(REP_A)], axis=0)
        k = kbuf[blk * BLOCK:(blk + 3) * BLOCK, hk * HEAD_DIM_A:(hk + 1) * HEAD_DIM_A]
        return _dot_nt(q, k)

    pending = [scores(u) for u in units[:ATTN_LOOKAHEAD]]
    for i, (blk, hk) in enumerate(units):
        s = pending.pop(0)
        if i + ATTN_LOOKAHEAD < len(units):
            pending.append(scores(units[i + ATTN_LOOKAHEAD]))
        rows = slice(blk * BLOCK, (blk + 1) * BLOCK)
        krows = slice(blk * BLOCK, (blk + 3) * BLOCK)
        t = bias_ref[hk] + s
        t_p, t_c, t_n = t[:, 0:BLOCK], t[:, BLOCK:2 * BLOCK], t[:, 2 * BLOCK:]
        if blk == 0:
            t_p = t_p + edge_p
        if blk == nblk - 1:
            t_n = t_n + edge_n
        sink2 = jnp.concatenate(
            [jnp.full((BLOCK, BLOCK), sink_ref[hk * REP_A + r] * LOG2_E, jnp.float32)
             for r in range(REP_A)], axis=0)
        m = jnp.max(jnp.maximum(jnp.maximum(t_p, t_c), t_n), axis=-1, keepdims=True)
        m = jnp.maximum(jnp.broadcast_to(m, (REP_A * BLOCK, BLOCK)), sink2)
        p = jnp.concatenate([jnp.exp2(t_p - m), jnp.exp2(t_c - m), jnp.exp2(t_n - m)],
                            axis=1).astype(jnp.bfloat16)
        ov = _dot(p, vbuf[krows, 2 * hk * HEAD_DIM_A:(2 * hk + 2) * HEAD_DIM_A])
        denom = ov[:, HEAD_DIM_A:] + jnp.exp2(sink2 - m)
        o = ov[:, :HEAD_DIM_A] * (1.0 / denom)
        for r in range(REP_A):
            hsl = slice((hk * REP_A + r) * HEAD_DIM_A, (hk * REP_A + r + 1) * HEAD_DIM_A)
            z = z_ref[rows, hsl].astype(jnp.float32)
            o_ref[rows, hsl] = (o[r * BLOCK:(r + 1) * BLOCK] * z).astype(o_ref.dtype)


def _branch_a(q3, proj3, sink, tq, cast_weights):
    b, s, _ = proj3.shape
    nb = s // BLOCK
    per = tq // BLOCK
    nt = s // tq
    n_cast = len(cast_weights)
    steps_per_weight = (b * nt) // n_cast
    assert steps_per_weight * n_cast == b * nt

    def cast_spec(k, w):
        rows, cols = w.shape
        assert rows % steps_per_weight == 0
        def imap(bi, n, sink_ref):
            return (jnp.clip(bi * nt + n - k * steps_per_weight, 0, steps_per_weight - 1), 0)
        return pl.BlockSpec((rows // steps_per_weight, cols), imap)

    cast_specs = [cast_spec(k, w) for k, w in enumerate(cast_weights)]

    def tile(width, idx):
        return pl.BlockSpec((None, tq, width), lambda bi, n, sink_ref: (bi, n, idx))

    def halo(idx, after):
        def imap(bi, n, sink_ref):
            blk = (n + 1) * per if after else n * per - 1
            return (bi, jnp.clip(blk, 0, nb - 1), idx)
        return pl.BlockSpec((None, BLOCK, KV_WIDTH_A), imap)

    in_specs = [
        tile(WIDTH_A, 0),
        halo(OUT_K_A, False), tile(KV_WIDTH_A, OUT_K_A), halo(OUT_K_A, True),
        halo(OUT_V_A, False), tile(KV_WIDTH_A, OUT_V_A), halo(OUT_V_A, True),
        tile(WIDTH_A, OUT_Z_A_T),
    ] + cast_specs
    outs = pl.pallas_call(
        functools.partial(_attn_kernel, n_cast=n_cast),
        grid_spec=pltpu.PrefetchScalarGridSpec(
            num_scalar_prefetch=1,
            grid=(b, nt),
            in_specs=in_specs,
            out_specs=[pl.BlockSpec((None, tq, WIDTH_A), lambda bi, n, sink_ref: (bi, n, 0))]
            + cast_specs,
            scratch_shapes=[pltpu.VMEM((tq + 2 * BLOCK, KV_WIDTH_A), jnp.bfloat16),
                            pltpu.VMEM((tq + 2 * BLOCK, 2 * KV_WIDTH_A), jnp.bfloat16),
                            pltpu.VMEM((N_KV_HEADS_A, REP_A * BLOCK, 3 * BLOCK), jnp.float32)],
        ),
        out_shape=[jax.ShapeDtypeStruct((b, s, WIDTH_A), jnp.bfloat16)]
        + [jax.ShapeDtypeStruct(w.shape, jnp.bfloat16) for w in cast_weights],
        compiler_params=pltpu.CompilerParams(dimension_semantics=("arbitrary", "arbitrary"),
                                             vmem_limit_bytes=VMEM_LIMIT),
        name="branch_a_attn",
    )(sink, q3, *([proj3] * 7), *cast_weights)
    return outs[0], outs[1:]


def _gmlp_spatial(v_ref, lng_ref, lnb_ref, ws_ref):
    nc = v_ref.shape[0] // CHUNK
    gv = v_ref[...].astype(jnp.float32)
    mu = jnp.mean(gv, axis=-1, keepdims=True)
    xc = gv - mu
    var = jnp.mean(xc * xc, axis=-1, keepdims=True)
    vn = (xc * lax.rsqrt(var + EPS) * lng_ref[...] + lnb_ref[...]).astype(jnp.bfloat16)
    out = []
    for g in range(N_GROUPS_B):
        gsl = slice(g * CHUNK, (g + 1) * CHUNK)
        rhs = jnp.concatenate([vn[c * CHUNK:(c + 1) * CHUNK, gsl] for c in range(nc)], axis=1)
        out.append(_dot(ws_ref[g], rhs))
    return out


def _gmlp_gate(spatial, u_ref, z_ref, bs_ref, o_ref):
    nc = o_ref.shape[0] // CHUNK
    for g, sg in enumerate(spatial):
        gsl = slice(g * CHUNK, (g + 1) * CHUNK)
        bias = bs_ref[g]
        for c in range(nc):
            rows = slice(c * CHUNK, (c + 1) * CHUNK)
            s_c = sg[:, c * CHUNK:(c + 1) * CHUNK] + bias
            u = u_ref[rows, gsl].astype(jnp.float32)
            z = z_ref[rows, gsl].astype(jnp.float32)
            o_ref[rows, gsl] = (u * s_c * z).astype(o_ref.dtype)


def _mem_kv_kernel(mem_ref, g_ref, w_ref, o_ref, memh_s):
    @pl.when(pl.program_id(0) == 0)
    def _():
        memh_s[...] = _rmsnorm_rows(mem_ref[...], g_ref[...]).astype(memh_s.dtype)

    o_ref[...] = _dot(memh_s[...], w_ref[...].astype(jnp.bfloat16)).astype(o_ref.dtype)


def _mem_kv(mem2d, gain, w_kv, tn):
    m, d = mem2d.shape
    n = w_kv.shape[1]
    return pl.pallas_call(
        _mem_kv_kernel,
        grid=(n // tn,),
        in_specs=[pl.BlockSpec((m, d), lambda j: (0, 0), pipeline_mode=pl.Buffered(1)),
                  pl.BlockSpec((1, d), lambda j: (0, 0)),
                  pl.BlockSpec((d, tn), lambda j: (0, j))],
        out_specs=pl.BlockSpec((m, tn), lambda j: (0, j)),
        out_shape=jax.ShapeDtypeStruct((m, n), jnp.bfloat16),
        scratch_shapes=[pltpu.VMEM((m, d), jnp.bfloat16)],
        compiler_params=pltpu.CompilerParams(dimension_semantics=("arbitrary",),
                                             vmem_limit_bytes=VMEM_LIMIT),
        name="mem_kv",
    )(mem2d, gain.reshape(1, d), w_kv)


def _xattn_scores(q_refs, kv_ref):
    out = []
    for h in range(N_HEADS_C):
        cb, off = divmod(h * HEAD_DIM_C, COL_BLOCK)
        q = q_refs[cb][:, off:off + HEAD_DIM_C]
        out.append(_dot_nt(q, kv_ref[:, h * HEAD_DIM_C:(h + 1) * HEAD_DIM_C]))
    return out


def _xattn_values(scores, kv_ref):
    scale = 1.0 / math.sqrt(HEAD_DIM_C)
    out = []
    for h, s in enumerate(scores):
        s = s * scale
        m = jnp.max(s, axis=-1, keepdims=True)
        p = jnp.exp(s - m)
        denom = jnp.sum(p, axis=-1, keepdims=True)
        v = kv_ref[:, WIDTH_C + h * HEAD_DIM_C:WIDTH_C + (h + 1) * HEAD_DIM_C]
        out.append((_dot(p.astype(jnp.bfloat16), v), denom))
    return out


def _xattn_finish(values, z_refs, o_ref):
    for h, (o, denom) in enumerate(values):
        cb, off = divmod(h * HEAD_DIM_C, COL_BLOCK)
        z = z_refs[cb][:, off:off + HEAD_DIM_C].astype(jnp.float32)
        o_ref[:, h * HEAD_DIM_C:(h + 1) * HEAD_DIM_C] = ((o / denom) * z).astype(o_ref.dtype)


def _tail_kernel(x_ref, ga_ref, g0_ref, g1_ref, g2_ref, g3_ref,
                 u_ref, v_ref, zb_ref, qc0_ref, qc1_ref, zc0_ref, zc1_ref, kv_ref,
                 lng_ref, lnb_ref, ws_ref, bs_ref, wa_ref, wb_ref, wc_ref, wo_ref, fg_ref,
                 o_ref, gb_s, gc_s, *, final_norm):
    d = D_MODEL
    g_refs = (g0_ref, g1_ref, g2_ref, g3_ref)

    def gate(branch):
        parts = []
        for cb in range(branch * D_MODEL // COL_BLOCK, (branch + 1) * D_MODEL // COL_BLOCK):
            blk, off = divmod(cb * COL_BLOCK, TRIPLE)
            parts.append(g_refs[blk][:, off:off + COL_BLOCK])
        return jnp.concatenate(parts, axis=1).astype(jnp.float32)

    half = d // 2
    ga = ga_ref[...]
    xs = _xattn_scores((qc0_ref, qc1_ref), kv_ref)
    a_lo = _dot(ga, wa_ref[:, :half])
    spatial = _gmlp_spatial(v_ref, lng_ref, lnb_ref, ws_ref)
    a_hi = _dot(ga, wa_ref[:, half:])
    xv = _xattn_values(xs, kv_ref)
    merged = jnp.concatenate([a_lo, a_hi], axis=1) * gate(0)
    _gmlp_gate(spatial, u_ref, zb_ref, bs_ref, gb_s)
    merged = merged + _dot(gb_s[...], wb_ref[...]) * gate(1)
    _xattn_finish(xv, (zc0_ref, zc1_ref), gc_s)
    merged = merged + _dot(gc_s[...], wc_ref[...]) * gate(2)
    y = x_ref[...] + _dot(merged.astype(jnp.bfloat16), wo_ref[...])
    if final_norm:
        ms = jnp.mean(y * y, axis=-1, keepdims=True)
        y = (y * lax.rsqrt(ms + EPS)) * fg_ref[...]
    o_ref[...] = y


def _tail(x3, ga3, proj3, kv3, ln_gain, ln_bias, w_spatial_bf16, b_spatial_bcast,
          wa, wb, wc, wo, final_gain, tm, final_norm):
    b, s, d = x3.shape

    def rows(width, idx=0):
        return pl.BlockSpec((None, tm, width), lambda bi, i: (bi, i, idx))

    def resident(shape):
        zeros = (0,) * len(shape)
        return pl.BlockSpec(shape, lambda bi, i: zeros, pipeline_mode=pl.Buffered(1))

    in_specs = [
        rows(d), rows(WIDTH_A),
        rows(TRIPLE, OUT_GATES_T[0]), rows(TRIPLE, OUT_GATES_T[1]),
        rows(TRIPLE, OUT_GATES_T[2]), rows(TRIPLE, OUT_GATES_T[3]),
        rows(WIDTH_B, OUT_U_B_T), rows(WIDTH_B, OUT_V_B_T), rows(WIDTH_B, OUT_Z_B_T),
        rows(COL_BLOCK, OUT_Q_C[0]), rows(COL_BLOCK, OUT_Q_C[1]),
        rows(COL_BLOCK, OUT_Z_C[0]), rows(COL_BLOCK, OUT_Z_C[1]),
        pl.BlockSpec((None, MEM_LEN, 2 * WIDTH_C), lambda bi, i: (bi, 0, 0)),
        resident((1, WIDTH_B)), resident((1, WIDTH_B)),
        resident((N_GROUPS_B, CHUNK, CHUNK)), resident((N_GROUPS_B, CHUNK, CHUNK)),
        resident((WIDTH_A, d)), resident((WIDTH_B, d)), resident((WIDTH_C, d)), resident((d, d)),
        resident((1, d)),
    ]
    return pl.pallas_call(
        functools.partial(_tail_kernel, final_norm=final_norm),
        grid=(b, s // tm),
        in_specs=in_specs,
        out_specs=rows(d),
        out_shape=jax.ShapeDtypeStruct((b, s, d), jnp.float32),
        scratch_shapes=[pltpu.VMEM((tm, WIDTH_B), jnp.bfloat16),
                        pltpu.VMEM((tm, WIDTH_C), jnp.bfloat16)],
        compiler_params=pltpu.CompilerParams(dimension_semantics=("arbitrary", "arbitrary"),
                                             vmem_limit_bytes=TAIL_VMEM_LIMIT),
        name="tail_merge_out",
    )(x3, ga3, *([proj3] * 4), *([proj3] * 7), kv3,
      ln_gain.reshape(1, WIDTH_B), ln_bias.reshape(1, WIDTH_B), w_spatial_bf16, b_spatial_bcast,
      wa, wb, wc, wo, final_gain.reshape(1, d))


def kernel(x, mem, norm_gain, mem_norm_gain, w_in, sink, ln_v_gain, ln_v_bias, w_spatial, b_spatial,
           w_kv_mem, w_br_a, w_br_b, w_br_c, w_out, final_gain):
    b, s, d = x.shape
    depth = w_in.shape[0]
    bf16 = jnp.bfloat16
    x2d = x.reshape(b * s, d)
    mem2d = mem.reshape(b * MEM_LEN, d)
    for l in range(depth):
        h, q = _qproj(x2d, norm_gain[l], w_in[l], QPROJ_ROWS, QPROJ_CHUNK_ROWS)
        kv = _mem_kv(mem2d, mem_norm_gain[l], w_kv_mem[l], MEM_KV_COLS)
        proj = _inproj(h, w_in[l], PROJ_SRC_BLOCKS, PROJ_STEP_PATTERN, PROJ_PATTERNS,
                       INPROJ_ROWS, INPROJ_CHUNK_ROWS)
        proj3 = proj.reshape(b, s, N_PROJ)

        ga3, (wa, wb, wc, wo) = _branch_a(q.reshape(b, s, WIDTH_A), proj3,
                                          sink[l].astype(jnp.float32), ATTN_ROWS,
                                          (w_br_a[l], w_br_b[l], w_br_c[l], w_out[l]))

        bs_bcast = jnp.broadcast_to(b_spatial[l].astype(jnp.float32)[:, :, None],
                                    (N_GROUPS_B, CHUNK, CHUNK))
        x3 = _tail(x2d.reshape(b, s, d), ga3, proj3,
                   kv.reshape(b, MEM_LEN, 2 * WIDTH_C), ln_v_gain[l], ln_v_bias[l],
                   w_spatial[l].astype(bf16), bs_bcast, wa, wb, wc, wo,
                   final_gain, TAIL_ROWS, final_norm=(l == depth - 1))
        x2d = x3.reshape(b * s, d)
    return x2d.reshape(b, s, d)
```

```python
import functools
import math

import jax
import jax.numpy as jnp
import numpy as np
from jax import lax
from jax.experimental import pallas as pl
from jax.experimental.pallas import tpu as pltpu

D_MODEL = 2048
MEM_LEN = 256
EPS = 1e-6
HEAD_DIM_A = 128
WIDTH_A = 1536
N_Q_HEADS_A = 12
N_KV_HEADS_A = 4
REP_A = N_Q_HEADS_A // N_KV_HEADS_A
KV_WIDTH_A = 512
BLOCK = 128
WIDTH_B = 1536
CHUNK = 128
N_GROUPS_B = 12
N_HEADS_C = 4
WIDTH_C = 1024
HEAD_DIM_C = 256
NEG_INF = -1e30

SQRT_HALF = math.sqrt(0.5)
LOG2_E = math.log2(math.e)

MIB = 1024 * 1024
V7X_VMEM_BYTES = 64 * MIB
VMEM_LIMIT = V7X_VMEM_BYTES - 8 * MIB
TAIL_VMEM_LIMIT = V7X_VMEM_BYTES - 1 * MIB

QPROJ_ROWS = 1024
QPROJ_CHUNK_ROWS = 256
INPROJ_ROWS = 2048
INPROJ_CHUNK_ROWS = 512
ATTN_ROWS = 512
TAIL_ROWS = 512
OUTPROJ_ROWS = 1024
OUTPROJ_CHUNK_ROWS = 256
MEM_KV_COLS = 512
QK_LOG2_SCALE = LOG2_E / math.sqrt(HEAD_DIM_A)

COL_BLOCK = 512
TRIPLE = 3 * COL_BLOCK
PROJ_SRC_BLOCKS = (3, 4, 17,
                   5, 6, 7,
                   14, 15, 16,
                   8, 9, 10,
                   11, 12, 13,
                   18, 19, 20
                   ) + tuple(range(21, 33))
PROJ_PATTERNS = (("plain",) * 3, ("silu",) * 3, ("gelu",) * 3, ("plain", "silu", "silu"),
                 ("sigmoid",) * 3)
PROJ_STEP_PATTERN = (0, 1, 1, 2, 2, 3, 4, 4, 4, 4)
N_PROJ = len(PROJ_SRC_BLOCKS) * COL_BLOCK
INPROJ_CHUNKS_PER_ITER = 2
ATTN_LOOKAHEAD = 1
OUT_Z_A_T, OUT_Z_B_T, OUT_U_B_T, OUT_V_B_T = 1, 2, 3, 4
OUT_GATES_T = (6, 7, 8, 9)
OUT_K_A, OUT_V_A = 0, 1
OUT_Q_C = (2, 15)
OUT_Z_C = (16, 17)


def _alibi_slopes(n):
    def pow2_slopes(m):
        start = 2.0 ** (-8.0 / m)
        return [start ** (i + 1) for i in range(m)]
    if math.log2(n).is_integer():
        s = pow2_slopes(n)
    else:
        c = 2 ** int(math.floor(math.log2(n)))
        s = pow2_slopes(c) + pow2_slopes(2 * c)[0::2][: n - c]
    return [float(v) for v in np.asarray(s, dtype=np.float32)]


ALIBI_SLOPES = _alibi_slopes(N_Q_HEADS_A)


def _sigmoid(x):
    return 0.5 * (1.0 + jnp.tanh(0.5 * x))


def _activation(x, kind):
    if kind == "plain":
        return x
    if kind == "silu":
        return x * _sigmoid(x)
    if kind == "gelu":
        return 0.5 * x * (1.0 + lax.erf(x * SQRT_HALF))
    if kind == "sigmoid":
        return _sigmoid(x)
    raise ValueError(kind)


def _aligned(index, multiple):
    return index if isinstance(index, int) else pl.multiple_of(index, multiple)


def _dot(a, b):
    return jnp.dot(a, b, preferred_element_type=jnp.float32)


def _dot_nt(a, b):
    return lax.dot_general(a, b, (((1,), (1,)), ((), ())), preferred_element_type=jnp.float32)


def _rmsnorm_rows(x, gain):
    ms = jnp.mean(x * x, axis=-1, keepdims=True)
    return (x * lax.rsqrt(ms + EPS)) * gain


def _qproj_kernel(x_ref, g_ref, w_ref, h_ref, q_ref, wbf_s, *, chunk):
    @pl.when(pl.program_id(0) == 0)
    def _():
        wbf_s[...] = w_ref[...].astype(wbf_s.dtype)

    for c in range(x_ref.shape[0] // chunk):
        rows = slice(c * chunk, (c + 1) * chunk)
        h = _rmsnorm_rows(x_ref[rows, :], g_ref[...]).astype(h_ref.dtype)
        h_ref[rows, :] = h
        q_ref[rows, :] = (_dot(h, wbf_s[...]) * QK_LOG2_SCALE).astype(q_ref.dtype)


def _qproj(x2d, gain, w_in, tm, chunk):
    t, d = x2d.shape
    return pl.pallas_call(
        functools.partial(_qproj_kernel, chunk=chunk),
        grid=(t // tm,),
        in_specs=[pl.BlockSpec((tm, d), lambda i: (i, 0)),
                  pl.BlockSpec((1, d), lambda i: (0, 0)),
                  pl.BlockSpec((d, WIDTH_A), lambda i: (0, 0), pipeline_mode=pl.Buffered(1))],
        out_specs=[pl.BlockSpec((tm, d), lambda i: (i, 0)),
                   pl.BlockSpec((tm, WIDTH_A), lambda i: (i, 0))],
        out_shape=[jax.ShapeDtypeStruct((t, d), jnp.bfloat16),
                   jax.ShapeDtypeStruct((t, WIDTH_A), jnp.bfloat16)],
        scratch_shapes=[pltpu.VMEM((d, WIDTH_A), jnp.bfloat16)],
        compiler_params=pltpu.CompilerParams(dimension_semantics=("arbitrary",),
                                             vmem_limit_bytes=VMEM_LIMIT),
        name="rmsnorm_qproj",
    )(x2d, gain.reshape(1, d), w_in)


def _inproj_kernel(src_ref, pat_ref, h_ref, w0_ref, w1_ref, w2_ref, o_ref, wbuf, *, patterns, chunk):
    jp = pl.program_id(0)
    i = pl.program_id(1)
    tm = h_ref.shape[0]
    slice_rows = w0_ref.shape[0]
    nxt = jp % 2
    cur = (jp + 1) % 2

    def cast_next_weights(part, n_parts):
        part_rows = slice_rows // n_parts
        src_rows = pl.ds(_aligned(part * part_rows, part_rows), part_rows)
        dst_rows = pl.ds(pl.multiple_of(i * slice_rows + part * part_rows, part_rows), part_rows)
        for s, w_ref in enumerate((w0_ref, w1_ref, w2_ref)):
            wbuf[nxt, dst_rows, s * COL_BLOCK:(s + 1) * COL_BLOCK] = (
                w_ref[src_rows, :].astype(wbuf.dtype))

    @pl.when(jp == 0)
    def _():
        cast_next_weights(0, 1)

    pat = pat_ref[jnp.maximum(jp - 1, 0)]
    for p, pattern in enumerate(patterns):
        @pl.when((jp > 0) & (pat == p))
        def _(pattern=pattern):
            n_iter = tm // (INPROJ_CHUNKS_PER_ITER * chunk)

            def body(it, carry):
                cast_next_weights(it, n_iter)
                for c in range(INPROJ_CHUNKS_PER_ITER):
                    start = _aligned((it * INPROJ_CHUNKS_PER_ITER + c) * chunk, chunk)
                    rows = pl.ds(start, chunk)
                    hc = h_ref[rows, :]
                    for s in range(3):
                        acc = _dot(hc, wbuf[cur, :, s * COL_BLOCK:(s + 1) * COL_BLOCK])
                        o_ref[rows, s * COL_BLOCK:(s + 1) * COL_BLOCK] = _activation(
                            acc, pattern[s]).astype(o_ref.dtype)
                return carry

            if n_iter == 1:
                body(0, 0)
            else:
                lax.fori_loop(0, n_iter, body, 0)


def _inproj(h, w_in, src_blocks, step_pattern, patterns, tm, chunk):
    t, d = h.shape
    n_steps = len(step_pattern)
    m_tiles = t // tm
    assert len(src_blocks) == 3 * n_steps and d % m_tiles == 0
    slice_rows = d // m_tiles

    def w_spec(s):
        def imap(jp, i, src_ref, pat_ref):
            return (i, src_ref[3 * jnp.minimum(jp, n_steps - 1) + s])
        return pl.BlockSpec((slice_rows, COL_BLOCK), imap)

    def h_map(jp, i, src_ref, pat_ref):
        return (jnp.where(jp == 0, 0, i), 0)

    def o_map(jp, i, src_ref, pat_ref):
        return (jnp.where(jp == 0, 0, i), jnp.maximum(jp - 1, 0))

    return pl.pallas_call(
        functools.partial(_inproj_kernel, patterns=patterns, chunk=chunk),
        grid_spec=pltpu.PrefetchScalarGridSpec(
            num_scalar_prefetch=2,
            grid=(n_steps + 1, m_tiles),
            in_specs=[pl.BlockSpec((tm, d), h_map), w_spec(0), w_spec(1), w_spec(2)],
            out_specs=pl.BlockSpec((tm, TRIPLE), o_map),
            scratch_shapes=[pltpu.VMEM((2, d, TRIPLE), jnp.bfloat16)],
        ),
        out_shape=jax.ShapeDtypeStruct((t, n_steps * TRIPLE), jnp.bfloat16),
        compiler_params=pltpu.CompilerParams(dimension_semantics=("arbitrary", "arbitrary"),
                                             vmem_limit_bytes=VMEM_LIMIT),
        name="inproj",
    )(jnp.asarray(src_blocks, jnp.int32), jnp.asarray(step_pattern, jnp.int32),
      h, w_in, w_in, w_in)


def _attn_kernel(sink_ref, q_ref, kp_ref, kc_ref, kn_ref, vp_ref, vc_ref, vn_ref, z_ref, *rest,
                 n_cast):
    w_refs = rest[:n_cast]
    o_ref = rest[n_cast]
    wo_refs = rest[n_cast + 1:2 * n_cast + 1]
    kbuf, vbuf, bias_ref = rest[2 * n_cast + 1:]
    tq = q_ref.shape[0]
    nblk = tq // BLOCK
    bi = pl.program_id(0)
    n = pl.program_id(1)
    nt = pl.num_programs(1)

    if n_cast:
        step = bi * nt + n
        steps_per_weight = (pl.num_programs(0) * nt) // n_cast
        for k in range(n_cast):
            @pl.when((step >= k * steps_per_weight) & (step < (k + 1) * steps_per_weight))
            def _(k=k):
                wo_refs[k][...] = w_refs[k][...].astype(wo_refs[k].dtype)

    @pl.when((bi == 0) & (n == 0))
    def _():
        row = lax.broadcasted_iota(jnp.int32, (BLOCK, 3 * BLOCK), 0)
        col = lax.broadcasted_iota(jnp.int32, (BLOCK, 3 * BLOCK), 1)
        dist = jnp.abs(row - (col - BLOCK))
        valid = dist <= BLOCK
        distf = dist.astype(jnp.float32)
        for h in range(N_Q_HEADS_A):
            hk, r = divmod(h, REP_A)
            bias_ref[hk, r * BLOCK:(r + 1) * BLOCK, :] = jnp.where(
                valid, (-ALIBI_SLOPES[h] * LOG2_E) * distf, NEG_INF)
        for hk in range(N_KV_HEADS_A):
            vbuf[:, (2 * hk + 1) * HEAD_DIM_A:(2 * hk + 2) * HEAD_DIM_A] = jnp.ones(
                (tq + 2 * BLOCK, HEAD_DIM_A), vbuf.dtype)

    kbuf[0:BLOCK, :] = kp_ref[...]
    kbuf[BLOCK:BLOCK + tq, :] = kc_ref[...]
    kbuf[BLOCK + tq:, :] = kn_ref[...]
    for hk in range(N_KV_HEADS_A):
        src = slice(hk * HEAD_DIM_A, (hk + 1) * HEAD_DIM_A)
        dst = slice(2 * hk * HEAD_DIM_A, (2 * hk + 1) * HEAD_DIM_A)
        vbuf[0:BLOCK, dst] = vp_ref[:, src]
        vbuf[BLOCK:BLOCK + tq, dst] = vc_ref[:, src]
        vbuf[BLOCK + tq:, dst] = vn_ref[:, src]

    edge_p = jnp.where(n == 0, NEG_INF, 0.0)
    edge_n = jnp.where(n == nt - 1, NEG_INF, 0.0)

    units = [(blk, hk) for blk in range(nblk) for hk in range(N_KV_HEADS_A)]

    def scores(unit):
        blk, hk = unit
        rows = slice(blk * BLOCK, (blk + 1) * BLOCK)
        q = jnp.concatenate(
            [q_ref[rows, (hk * REP_A + r) * HEAD_DIM_A:(hk * REP_A + r + 1) * HEAD_DIM_A]
             for r in range(REP_A)], axis=0)
        k = kbuf[blk * BLOCK:(blk + 3) * BLOCK, hk * HEAD_DIM_A:(hk + 1) * HEAD_DIM_A]
        return _dot_nt(q, k)

    pending = [scores(u) for u in units[:ATTN_LOOKAHEAD]]
    for i, (blk, hk) in enumerate(units):
        s = pending.pop(0)
        if i + ATTN_LOOKAHEAD < len(units):
            pending.append(scores(units[i + ATTN_LOOKAHEAD]))
        rows = slice(blk * BLOCK, (blk + 1) * BLOCK)
        krows = slice(blk * BLOCK, (blk + 3) * BLOCK)
        t = bias_ref[hk] + s
        t_p, t_c, t_n = t[:, 0:BLOCK], t[:, BLOCK:2 * BLOCK], t[:, 2 * BLOCK:]
        if blk == 0:
            t_p = t_p + edge_p
        if blk == nblk - 1:
            t_n = t_n + edge_n
        sink2 = jnp.concatenate(
            [jnp.full((BLOCK, BLOCK), sink_ref[hk * REP_A + r] * LOG2_E, jnp.float32)
             for r in range(REP_A)], axis=0)
        m = jnp.max(jnp.maximum(jnp.maximum(t_p, t_c), t_n), axis=-1, keepdims=True)
        m = jnp.maximum(jnp.broadcast_to(m, (REP_A * BLOCK, BLOCK)), sink2)
        p = jnp.concatenate([jnp.exp2(t_p - m), jnp.exp2(t_c - m), jnp.exp2(t_n - m)],
                            axis=1).astype(jnp.bfloat16)
        ov = _dot(p, vbuf[krows, 2 * hk * HEAD_DIM_A:(2 * hk + 2) * HEAD_DIM_A])
        denom = ov[:, HEAD_DIM_A:] + jnp.exp2(sink2 - m)
        o = ov[:, :HEAD_DIM_A] * (1.0 / denom)
        for r in range(REP_A):
            hsl = slice((hk * REP_A + r) * HEAD_DIM_A, (hk * REP_A + r + 1) * HEAD_DIM_A)
            z = z_ref[rows, hsl].astype(jnp.float32)
            o_ref[rows, hsl] = (o[r * BLOCK:(r + 1) * BLOCK] * z).astype(o_ref.dtype)


def _branch_a(q3, proj3, sink, tq, cast_weights):
    b, s, _ = proj3.shape
    nb = s // BLOCK
    per = tq // BLOCK
    nt = s // tq
    n_cast = len(cast_weights)
    steps_per_weight = (b * nt) // n_cast
    assert steps_per_weight * n_cast == b * nt

    def cast_spec(k, w):
        rows, cols = w.shape
        assert rows % steps_per_weight == 0
        def imap(bi, n, sink_ref):
            return (jnp.clip(bi * nt + n - k * steps_per_weight, 0, steps_per_weight - 1), 0)
        return pl.BlockSpec((rows // steps_per_weight, cols), imap)

    cast_specs = [cast_spec(k, w) for k, w in enumerate(cast_weights)]

    def tile(width, idx):
        return pl.BlockSpec((None, tq, width), lambda bi, n, sink_ref: (bi, n, idx))

    def halo(idx, after):
        def imap(bi, n, sink_ref):
            blk = (n + 1) * per if after else n * per - 1
            return (bi, jnp.clip(blk, 0, nb - 1), idx)
        return pl.BlockSpec((None, BLOCK, KV_WIDTH_A), imap)

    in_specs = [
        tile(WIDTH_A, 0),
        halo(OUT_K_A, False), tile(KV_WIDTH_A, OUT_K_A), halo(OUT_K_A, True),
        halo(OUT_V_A, False), tile(KV_WIDTH_A, OUT_V_A), halo(OUT_V_A, True),
        tile(WIDTH_A, OUT_Z_A_T),
    ] + cast_specs
    outs = pl.pallas_call(
        functools.partial(_attn_kernel, n_cast=n_cast),
        grid_spec=pltpu.PrefetchScalarGridSpec(
            num_scalar_prefetch=1,
            grid=(b, nt),
            in_specs=in_specs,
            out_specs=[pl.BlockSpec((None, tq, WIDTH_A), lambda bi, n, sink_ref: (bi, n, 0))]
            + cast_specs,
            scratch_shapes=[pltpu.VMEM((tq + 2 * BLOCK, KV_WIDTH_A), jnp.bfloat16),
                            pltpu.VMEM((tq + 2 * BLOCK, 2 * KV_WIDTH_A), jnp.bfloat16),
                            pltpu.VMEM((N_KV_HEADS_A, REP_A * BLOCK, 3 * BLOCK), jnp.float32)],
        ),
        out_shape=[jax.ShapeDtypeStruct((b, s, WIDTH_A), jnp.bfloat16)]
        + [jax.ShapeDtypeStruct(w.shape, jnp.bfloat16) for w in cast_weights],
        compiler_params=pltpu.CompilerParams(dimension_semantics=("arbitrary", "arbitrary"),
                                             vmem_limit_bytes=VMEM_LIMIT),
        name="branch_a_attn",
    )(sink, q3, *([proj3] * 7), *cast_weights)
    return outs[0], outs[1:]


def _gmlp_spatial(v_ref, lng_ref, lnb_ref, ws_ref):
    nc = v_ref.shape[0] // CHUNK
    gv = v_ref[...].astype(jnp.float32)
    mu = jnp.mean(gv, axis=-1, keepdims=True)
    xc = gv - mu
    var = jnp.mean(xc * xc, axis=-1, keepdims=True)
    vn = (xc * lax.rsqrt(var + EPS) * lng_ref[...] + lnb_ref[...]).astype(jnp.bfloat16)
    out = []
    for g in range(N_GROUPS_B):
        gsl = slice(g * CHUNK, (g + 1) * CHUNK)
        rhs = jnp.concatenate([vn[c * CHUNK:(c + 1) * CHUNK, gsl] for c in range(nc)], axis=1)
        out.append(_dot(ws_ref[g], rhs))
    return out


def _gmlp_gate(spatial, u_ref, z_ref, bs_ref, o_ref):
    nc = o_ref.shape[0] // CHUNK
    for g, sg in enumerate(spatial):
        gsl = slice(g * CHUNK, (g + 1) * CHUNK)
        bias = bs_ref[g]
        for c in range(nc):
            rows = slice(c * CHUNK, (c + 1) * CHUNK)
            s_c = sg[:, c * CHUNK:(c + 1) * CHUNK] + bias
            u = u_ref[rows, gsl].astype(jnp.float32)
            z = z_ref[rows, gsl].astype(jnp.float32)
            o_ref[rows, gsl] = (u * s_c * z).astype(o_ref.dtype)


def _mem_kv_kernel(mem_ref, g_ref, w_ref, o_ref, memh_s):
    @pl.when(pl.program_id(0) == 0)
    def _():
        memh_s[...] = _rmsnorm_rows(mem_ref[...], g_ref[...]).astype(memh_s.dtype)

    o_ref[...] = _dot(memh_s[...], w_ref[...].astype(jnp.bfloat16)).astype(o_ref.dtype)


def _mem_kv(mem2d, gain, w_kv, tn):
    m, d = mem2d.shape
    n = w_kv.shape[1]
    return pl.pallas_call(
        _mem_kv_kernel,
        grid=(n // tn,),
        in_specs=[pl.BlockSpec((m, d), lambda j: (0, 0), pipeline_mode=pl.Buffered(1)),
                  pl.BlockSpec((1, d), lambda j: (0, 0)),
                  pl.BlockSpec((d, tn), lambda j: (0, j))],
        out_specs=pl.BlockSpec((m, tn), lambda j: (0, j)),
        out_shape=jax.ShapeDtypeStruct((m, n), jnp.bfloat16),
        scratch_shapes=[pltpu.VMEM((m, d), jnp.bfloat16)],
        compiler_params=pltpu.CompilerParams(dimension_semantics=("arbitrary",),
                                             vmem_limit_bytes=VMEM_LIMIT),
        name="mem_kv",
    )(mem2d, gain.reshape(1, d), w_kv)


def _xattn_scores(q_refs, kv_ref):
    out = []
    for h in range(N_HEADS_C):
        cb, off = divmod(h * HEAD_DIM_C, COL_BLOCK)
        q = q_refs[cb][:, off:off + HEAD_DIM_C]
        out.append(_dot_nt(q, kv_ref[:, h * HEAD_DIM_C:(h + 1) * HEAD_DIM_C]))
    return out


def _xattn_values(scores, kv_ref):
    scale = 1.0 / math.sqrt(HEAD_DIM_C)
    out = []
    for h, s in enumerate(scores):
        s = s * scale
        m = jnp.max(s, axis=-1, keepdims=True)
        p = jnp.exp(s - m)
        denom = jnp.sum(p, axis=-1, keepdims=True)
        v = kv_ref[:, WIDTH_C + h * HEAD_DIM_C:WIDTH_C + (h + 1) * HEAD_DIM_C]
        out.append((_dot(p.astype(jnp.bfloat16), v), denom))
    return out


def _xattn_finish(values, z_refs, o_ref):
    for h, (o, denom) in enumerate(values):
        cb, off = divmod(h * HEAD_DIM_C, COL_BLOCK)
        z = z_refs[cb][:, off:off + HEAD_DIM_C].astype(jnp.float32)
        o_ref[:, h * HEAD_DIM_C:(h + 1) * HEAD_DIM_C] = ((o / denom) * z).astype(o_ref.dtype)


def _tail_kernel(ga_ref, g0_ref, g1_ref, g2_ref, g3_ref,
                 u_ref, v_ref, zb_ref, qc0_ref, qc1_ref, zc0_ref, zc1_ref, kv_ref,
                 lng_ref, lnb_ref, ws_ref, bs_ref, wa_ref, wb_ref, wc_ref,
                 o_ref, gb_s, gc_s):
    d = D_MODEL
    g_refs = (g0_ref, g1_ref, g2_ref, g3_ref)

    def gate(branch):
        parts = []
        for cb in range(branch * D_MODEL // COL_BLOCK, (branch + 1) * D_MODEL // COL_BLOCK):
            blk, off = divmod(cb * COL_BLOCK, TRIPLE)
            parts.append(g_refs[blk][:, off:off + COL_BLOCK])
        return jnp.concatenate(parts, axis=1).astype(jnp.float32)

    half = d // 2
    ga = ga_ref[...]
    xs = _xattn_scores((qc0_ref, qc1_ref), kv_ref)
    a_lo = _dot(ga, wa_ref[:, :half])
    spatial = _gmlp_spatial(v_ref, lng_ref, lnb_ref, ws_ref)
    a_hi = _dot(ga, wa_ref[:, half:])
    xv = _xattn_values(xs, kv_ref)
    merged = jnp.concatenate([a_lo, a_hi], axis=1) * gate(0)
    _gmlp_gate(spatial, u_ref, zb_ref, bs_ref, gb_s)
    merged = merged + _dot(gb_s[...], wb_ref[...]) * gate(1)
    _xattn_finish(xv, (zc0_ref, zc1_ref), gc_s)
    merged = merged + _dot(gc_s[...], wc_ref[...]) * gate(2)
    o_ref[...] = merged.astype(o_ref.dtype)


def _outproj_kernel(x_ref, m_ref, wo_ref, fg_ref, o_ref, *, chunk, final_norm):
    for c in range(x_ref.shape[0] // chunk):
        rows = slice(c * chunk, (c + 1) * chunk)
        y = x_ref[rows, :] + _dot(m_ref[rows, :], wo_ref[...])
        if final_norm:
            y = _rmsnorm_rows(y, fg_ref[...])
        o_ref[rows, :] = y


def _outproj(x2d, merged, wo, final_gain, tm, chunk, final_norm):
    t, d = x2d.shape
    rows = pl.BlockSpec((tm, d), lambda i: (i, 0))
    return pl.pallas_call(
        functools.partial(_outproj_kernel, chunk=chunk, final_norm=final_norm),
        grid=(t // tm,),
        in_specs=[rows, rows,
                  pl.BlockSpec((d, d), lambda i: (0, 0), pipeline_mode=pl.Buffered(1)),
                  pl.BlockSpec((1, d), lambda i: (0, 0))],
        out_specs=rows,
        out_shape=jax.ShapeDtypeStruct((t, d), jnp.float32),
        compiler_params=pltpu.CompilerParams(dimension_semantics=("arbitrary",),
                                             vmem_limit_bytes=VMEM_LIMIT),
        name="outproj_norm",
    )(x2d, merged, wo, final_gain.reshape(1, d))


def _tail(ga3, proj3, kv3, ln_gain, ln_bias, w_spatial_bf16, b_spatial_bcast, wa, wb, wc, tm):
    b, s, _ = ga3.shape
    d = D_MODEL

    def rows(width, idx=0):
        return pl.BlockSpec((None, tm, width), lambda bi, i: (bi, i, idx))

    def resident(shape):
        zeros = (0,) * len(shape)
        return pl.BlockSpec(shape, lambda bi, i: zeros, pipeline_mode=pl.Buffered(1))

    in_specs = [
        rows(WIDTH_A),
        rows(TRIPLE, OUT_GATES_T[0]), rows(TRIPLE, OUT_GATES_T[1]),
        rows(TRIPLE, OUT_GATES_T[2]), rows(TRIPLE, OUT_GATES_T[3]),
        rows(WIDTH_B, OUT_U_B_T), rows(WIDTH_B, OUT_V_B_T), rows(WIDTH_B, OUT_Z_B_T),
        rows(COL_BLOCK, OUT_Q_C[0]), rows(COL_BLOCK, OUT_Q_C[1]),
        rows(COL_BLOCK, OUT_Z_C[0]), rows(COL_BLOCK, OUT_Z_C[1]),
        pl.BlockSpec((None, MEM_LEN, 2 * WIDTH_C), lambda bi, i: (bi, 0, 0)),
        resident((1, WIDTH_B)), resident((1, WIDTH_B)),
        resident((N_GROUPS_B, CHUNK, CHUNK)), resident((N_GROUPS_B, CHUNK, CHUNK)),
        resident((WIDTH_A, d)), resident((WIDTH_B, d)), resident((WIDTH_C, d)),
    ]
    return pl.pallas_call(
        _tail_kernel,
        grid=(b, s // tm),
        in_specs=in_specs,
        out_specs=rows(d),
        out_shape=jax.ShapeDtypeStruct((b, s, d), jnp.bfloat16),
        scratch_shapes=[pltpu.VMEM((tm, WIDTH_B), jnp.bfloat16),
                        pltpu.VMEM((tm, WIDTH_C), jnp.bfloat16)],
        compiler_params=pltpu.CompilerParams(dimension_semantics=("arbitrary", "arbitrary"),
                                             vmem_limit_bytes=TAIL_VMEM_LIMIT),
        name="tail_merge_out",
    )(ga3, *([proj3] * 4), *([proj3] * 7), kv3,
      ln_gain.reshape(1, WIDTH_B), ln_bias.reshape(1, WIDTH_B), w_spatial_bf16, b_spatial_bcast,
      wa, wb, wc)


def kernel(x, mem, norm_gain, mem_norm_gain, w_in, sink, ln_v_gain, ln_v_bias, w_spatial, b_spatial,
           w_kv_mem, w_br_a, w_br_b, w_br_c, w_out, final_gain):
    b, s, d = x.shape
    depth = w_in.shape[0]
    bf16 = jnp.bfloat16
    x2d = x.reshape(b * s, d)
    mem2d = mem.reshape(b * MEM_LEN, d)
    for l in range(depth):
        h, q = _qproj(x2d, norm_gain[l], w_in[l], QPROJ_ROWS, QPROJ_CHUNK_ROWS)
        kv = _mem_kv(mem2d, mem_norm_gain[l], w_kv_mem[l], MEM_KV_COLS)
        proj = _inproj(h, w_in[l], PROJ_SRC_BLOCKS, PROJ_STEP_PATTERN, PROJ_PATTERNS,
                       INPROJ_ROWS, INPROJ_CHUNK_ROWS)
        proj3 = proj.reshape(b, s, N_PROJ)

        ga3, (wa, wb, wc, wo) = _branch_a(q.reshape(b, s, WIDTH_A), proj3,
                                          sink[l].astype(jnp.float32), ATTN_ROWS,
                                          (w_br_a[l], w_br_b[l], w_br_c[l], w_out[l]))

        bs_bcast = jnp.broadcast_to(b_spatial[l].astype(jnp.float32)[:, :, None],
                                    (N_GROUPS_B, CHUNK, CHUNK))
        merged = _tail(ga3, proj3, kv.reshape(b, MEM_LEN, 2 * WIDTH_C), ln_v_gain[l],
                       ln_v_bias[l], w_spatial[l].astype(bf16), bs_bcast, wa, wb, wc, TAIL_ROWS)
        x2d = _outproj(x2d, merged.reshape(b * s, d), wo, final_gain, OUTPROJ_ROWS,
                       OUTPROJ_CHUNK_ROWS, final_norm=(l == depth - 1))
    return x2d.reshape(b, s, d)
```
